```python
import math
import jax, jax.numpy as jnp
from jax import lax
import numpy as np

D_MODEL = 1024
BATCH = 32
SEQ = 256
DEPTH = 4
DEC_BATCH = 8
DEC_SEQ = 4096
PAST_LEN = 512

GRID_W = 64
POS_BASE = 10000.0
W_A = 256
H_A = 4
DK = 64
DV = 64
CONV_W = 4
DN_CHUNK = 64
W_B = 256
G_B = 4
SGU_CHUNK = 128
W_C = 256
POOL_WINDOWS = (2, 4, 8, 16)
C_C = 64
W_D = 256
G_D = 4
C_D = 64
MIX_W = W_A + W_B + W_C + W_D
SPLIT_SIZES = (3 * W_A, W_A, 2 * H_A, 2 * H_A, 2 * W_B, W_B, W_C, W_C, W_D, W_D)
P_IN = 3 * W_A + W_A + 4 * H_A + 3 * W_B + 2 * W_C + 2 * W_D
EPS = 1e-6

kernel_name = "hybrid_delta_sgu_pool_fourier_dit_step"


def _split_points():
    return [int(i) for i in np.cumsum(SPLIT_SIZES)[:-1]]


def rmsnorm(x, g):
    xf = x.astype(jnp.float32)
    y = xf * lax.rsqrt(jnp.mean(xf * xf, axis=-1, keepdims=True) + EPS)
    return y.astype(x.dtype) * g


def l2norm(x):
    xf = x.astype(jnp.float32)
    return xf * lax.rsqrt(jnp.sum(xf * xf, axis=-1, keepdims=True) + EPS)


def grid_pos_embed(T, dtype):
    rows = T // GRID_W
    r = jnp.repeat(jnp.arange(rows, dtype=jnp.float32), GRID_W)
    col = jnp.tile(jnp.arange(GRID_W, dtype=jnp.float32), rows)
    n_freq = D_MODEL // 4
    freqs = jnp.power(POS_BASE, -jnp.arange(n_freq, dtype=jnp.float32) / n_freq)
    ar = r[:, None] * freqs[None]
    ac = col[:, None] * freqs[None]
    return jnp.concatenate([jnp.sin(ar), jnp.cos(ar), jnp.sin(ac), jnp.cos(ac)], axis=-1).astype(dtype)


def short_conv(x, w):
    C = x.shape[-1]
    pad = (CONV_W // 2, CONV_W - 1 - CONV_W // 2)
    return lax.conv_general_dilated(x, w[:, None, :].astype(x.dtype), window_strides=(1,), padding=[pad],
                                    dimension_numbers=("NWC", "WIO", "NWC"), feature_group_count=C)


def chunk_delta_rule(q, k, v, beta, g, S0):
    B, T, H, _ = q.shape
    N = T // DN_CHUNK
    f32 = jnp.float32

    def chunks(a):
        a = a.astype(f32).reshape((B, N, DN_CHUNK) + a.shape[2:])
        return jnp.moveaxis(a, (1, 3), (0, 2))

    qc, kc, vc, bc, gc = chunks(q), chunks(k), chunks(v), chunks(beta), chunks(g)
    gcum = jnp.cumsum(gc, axis=-1)
    idx = jnp.arange(DN_CHUNK)
    incl = idx[:, None] >= idx[None, :]
    strict = idx[:, None] > idx[None, :]
    diff = gcum[..., :, None] - gcum[..., None, :]
    decay_mat = jnp.where(incl, jnp.exp(jnp.where(incl, diff, 0.0)), 0.0)
    kb = kc * bc[..., None]
    A = jnp.where(strict, jnp.einsum("nbhid,nbhjd->nbhij", kb, kc) * decay_mat, 0.0)
    eye = jnp.eye(DN_CHUNK, dtype=f32)
    Tinv = lax.linalg.triangular_solve(eye + A, jnp.broadcast_to(eye, A.shape), left_side=True, lower=True)
    u = jnp.einsum("nbhij,nbhje->nbhie", Tinv, vc * bc[..., None])
    w = jnp.einsum("nbhij,nbhjd->nbhid", Tinv, kb * jnp.exp(gcum)[..., None])
    qk = jnp.einsum("nbhid,nbhjd->nbhij", qc, kc) * decay_mat
    q_dec = qc * jnp.exp(gcum)[..., None]
    k_dec = kc * jnp.exp(gcum[..., -1:] - gcum)[..., None]
    g_last = jnp.exp(gcum[..., -1])

    def step(S, xs):
        q_i, k_i, u_i, w_i, qk_i, gl_i = xs
        v_new = u_i - jnp.einsum("bhcd,bhde->bhce", w_i, S)
        o_i = jnp.einsum("bhcd,bhde->bhce", q_i, S) + jnp.einsum("bhij,bhje->bhie", qk_i, v_new)
        S = S * gl_i[..., None, None] + jnp.einsum("bhcd,bhce->bhde", k_i, v_new)
        return S, o_i

    S_fin, o = lax.scan(step, S0.astype(f32), (q_dec, k_dec, u, w, qk, g_last))
    o = jnp.moveaxis(o, (0, 2), (1, 3)).reshape(B, T, H, o.shape[-1])
    return o.astype(q.dtype), S_fin.astype(q.dtype)


def deltanet_branch(qkv, g_gate, beta_logit, a_logit, S0, conv_qkv, a_log, dt_bias, dn_norm_g):
    B, T, _ = qkv.shape
    qkv = jax.nn.silu(short_conv(qkv, conv_qkv))
    q, k, v = jnp.split(qkv, 3, axis=-1)
    q = (l2norm(q.reshape(B, T, H_A, DK)) * (DK ** -0.5)).astype(qkv.dtype)
    k = l2norm(k.reshape(B, T, H_A, DK)).astype(qkv.dtype)
    v = v.reshape(B, T, H_A, DV)
    beta = jax.nn.sigmoid(beta_logit.reshape(B, T, 2, H_A))
    g = -jnp.exp(a_log.astype(jnp.float32)) * jax.nn.softplus(
        a_logit.reshape(B, T, 2, H_A).astype(jnp.float32) + dt_bias.astype(jnp.float32))
    o_f, S_f = chunk_delta_rule(q, k, v, beta[:, :, 0], g[:, :, 0], S0[:, 0])
    o_b, S_b = chunk_delta_rule(q[:, ::-1], k[:, ::-1], v[:, ::-1], beta[:, ::-1, 1], g[:, ::-1, 1], S0[:, 1])
    o = rmsnorm(o_f + o_b[:, ::-1], dn_norm_g)
    y = o.reshape(B, T, W_A) * jax.nn.silu(g_gate)
    return y, jnp.stack([S_f, S_b], axis=1)


def sgu_branch(uv, g_gate, sgu_norm_g, sgu_w, sgu_b):
    B, T, _ = uv.shape
    N = T // SGU_CHUNK
    u, v = jnp.split(jax.nn.gelu(uv), 2, axis=-1)
    v = rmsnorm(v, sgu_norm_g).reshape(B, N, SGU_CHUNK, G_B, W_B // G_B)
    sv = jnp.einsum("gpq,bnqgc->bnpgc", sgu_w, v) + sgu_b.T[None, None, :, :, None]
    return u * sv.reshape(B, T, W_B) * jax.nn.silu(g_gate)


def pool_branch(x_c, g_gate, pool_w, pool_scale):
    B, T, _ = x_c.shape
    xg = x_c.reshape(B, T, len(POOL_WINDOWS), C_C)
    cs = jnp.concatenate([jnp.zeros((B, 1, len(POOL_WINDOWS), C_C), jnp.float32),
                          jnp.cumsum(xg.astype(jnp.float32), axis=1)], axis=1)
    t = jnp.arange(T)
    means = []
    for gi, wsize in enumerate(POOL_WINDOWS):
        lo = jnp.clip(t - wsize // 2, 0, T)
        hi = jnp.clip(t + wsize - wsize // 2, 0, T)
        cs_g = cs[:, :, gi]
        means.append((cs_g[:, hi] - cs_g[:, lo]) / (hi - lo).astype(jnp.float32)[None, :, None])
    pooled = (jnp.stack(means, axis=2) - xg.astype(jnp.float32)).astype(x_c.dtype)
    y = jnp.einsum("btgc,gcd->btgd", pooled, pool_w).reshape(B, T, W_C)
    return y * pool_scale * jax.nn.silu(g_gate)


def fourier_branch(x_d, g_gate, fourier_w):
    B, T, _ = x_d.shape
    xg = x_d.reshape(B, T, G_D, C_D).astype(jnp.float32)
    f = jnp.fft.fftn(xg, axes=(1, 3), norm="ortho").real.astype(x_d.dtype)
    y = jnp.einsum("btgc,gcd->btgd", f, fourier_w).reshape(B, T, W_D)
    return y * jax.nn.silu(g_gate)


def trunk_layer(x, cond, S0, ada_w, ada_b, norm_g, w_in, conv_qkv, a_log, dt_bias, dn_norm_g,
                sgu_norm_g, sgu_w, sgu_b, pool_w, pool_scale, fourier_w, w_out):
    mod = jax.nn.silu(cond) @ ada_w + ada_b
    shift, scale, gate = jnp.split(mod, 3, axis=-1)
    h = rmsnorm(x, norm_g) * (1 + scale[:, None]) + shift[:, None]
    p = h @ w_in
    qkv, g_a, beta_logit, a_logit, uv, g_b, x_c, g_c, x_d, g_d = jnp.split(p, _split_points(), axis=-1)
    y_a, S_new = deltanet_branch(qkv, g_a, beta_logit, a_logit, S0, conv_qkv, a_log, dt_bias, dn_norm_g)
    y_b = sgu_branch(uv, g_b, sgu_norm_g, sgu_w, sgu_b)
    y_c = pool_branch(x_c, g_c, pool_w, pool_scale)
    y_d = fourier_branch(x_d, g_d, fourier_w)
    out = jnp.concatenate([y_a, y_b, y_c, y_d], axis=-1) @ w_out
    return x + gate[:, None] * out, S_new


def setup_inputs(seed: int = 0) -> dict:
    key = jax.random.key(seed)
    ks = jax.random.split(key, 24)
    f32 = jnp.float32

    def nrm(k, shape, s):
        return s * jax.random.normal(k, shape, f32)

    dt = jnp.exp(jax.random.uniform(ks[10], (DEPTH, 2, H_A), f32, minval=math.log(1e-3), maxval=math.log(1e-1)))
    return {
        "x_prompt": nrm(ks[0], (BATCH, SEQ, D_MODEL), 1.0),
        "x_sample": nrm(ks[1], (DEC_BATCH, DEC_SEQ, D_MODEL), 1.0),
        "state_delta": nrm(ks[2], (DEC_BATCH, DEPTH, 2, H_A, DK, DV), 1.0),
        "c": nrm(ks[3], (DEC_BATCH, D_MODEL), 1.0),
        "c_ctx": nrm(ks[4], (D_MODEL,), 1.0),
        "ada_w": nrm(ks[5], (DEPTH, D_MODEL, 3 * D_MODEL), 0.5 * D_MODEL ** -0.5),
        "ada_b": nrm(ks[6], (DEPTH, 3 * D_MODEL), 0.02),
        "norm_g": 1.0 + nrm(ks[7], (DEPTH, D_MODEL), 0.02),
        "w_in": nrm(ks[8], (DEPTH, D_MODEL, P_IN), D_MODEL ** -0.5),
        "conv_qkv": nrm(ks[9], (DEPTH, CONV_W, 3 * W_A), CONV_W ** -0.5),
        "a_log": jnp.log(jax.random.uniform(ks[11], (DEPTH, 2, H_A), f32, minval=1.0, maxval=16.0)),
        "dt_bias": dt + jnp.log(-jnp.expm1(-dt)),
        "dn_norm_g": 1.0 + nrm(ks[12], (DEPTH, DV), 0.02),
        "sgu_norm_g": 1.0 + nrm(ks[13], (DEPTH, W_B), 0.02),
        "sgu_w": nrm(ks[14], (DEPTH, G_B, SGU_CHUNK, SGU_CHUNK), SGU_CHUNK ** -0.5),
        "sgu_b": 1.0 + nrm(ks[15], (DEPTH, G_B, SGU_CHUNK), 0.02),
        "pool_w": nrm(ks[16], (DEPTH, len(POOL_WINDOWS), C_C, C_C), C_C ** -0.5),
        "pool_scale": 1.0 + nrm(ks[17], (DEPTH, W_C), 0.02),
        "fourier_w": nrm(ks[18], (DEPTH, G_D, C_D, C_D), C_D ** -0.5),
        "w_out": nrm(ks[19], (DEPTH, MIX_W, D_MODEL), MIX_W ** -0.5),
        "final_norm_g": 1.0 + nrm(ks[20], (D_MODEL,), 0.02),
    }


def reference(x_prompt, x_sample, state_delta, c, c_ctx, ada_w, ada_b, norm_g, w_in, conv_qkv, a_log, dt_bias,
              dn_norm_g, sgu_norm_g, sgu_w, sgu_b, pool_w, pool_scale, fourier_w, w_out, final_norm_g):
    per_layer = [dict(ada_w=ada_w[l], ada_b=ada_b[l], norm_g=norm_g[l], w_in=w_in[l], conv_qkv=conv_qkv[l],
                      a_log=a_log[l], dt_bias=dt_bias[l], dn_norm_g=dn_norm_g[l], sgu_norm_g=sgu_norm_g[l],
                      sgu_w=sgu_w[l], sgu_b=sgu_b[l], pool_w=pool_w[l], pool_scale=pool_scale[l],
                      fourier_w=fourier_w[l], w_out=w_out[l]) for l in range(DEPTH)]

    bp = x_prompt.shape[0]
    cond_ctx = jnp.broadcast_to(c_ctx, (bp, D_MODEL))
    xp = x_prompt
    ctx_states = []
    for l in range(DEPTH):
        S0 = jnp.zeros((bp, 2, H_A, DK, DV), x_prompt.dtype)
        xp, S_l = trunk_layer(xp, cond_ctx, S0, **per_layer[l])
        ctx_states.append(S_l)
    y_prompt = rmsnorm(xp, final_norm_g)
    new_state_delta = jnp.stack(ctx_states, axis=1)

    xs = x_sample + grid_pos_embed(x_sample.shape[1], x_sample.dtype)[None]
    for l in range(DEPTH):
        xs, _ = trunk_layer(xs, c, state_delta[:, l], **per_layer[l])
    y_sample = rmsnorm(xs, final_norm_g)
    return (y_prompt, y_sample, new_state_delta)
```

```python
import functools
import math

import numpy as np
import jax
import jax.numpy as jnp
from jax import lax
from jax.experimental import pallas as pl
from jax.experimental.pallas import tpu as pltpu

F32 = jnp.float32
BF16 = jnp.bfloat16

D_MODEL = 1024
DEPTH = 4
GRID_W = 64
POS_BASE = 10000.0
H_A = 4
DK = 64
W_A = 256
CONV_W = 4
DN_CHUNK = 64
W_B = 256
G_B = 4
SGU_CHUNK = 128
W_C = 256
POOL_WINDOWS = (2, 4, 8, 16)
W_D = 256
G_D = 4
C_D = 64
EPS = 1e-6

P_MAIN = 2816
P_PAD = P_MAIN + 128
HALO = 16
DREC = 5 * W_A

VMEM_LIMIT = 56 * 1024 * 1024


def _cparams(sem):
    return pltpu.CompilerParams(dimension_semantics=sem, vmem_limit_bytes=VMEM_LIMIT)


def _silu(x):
    return x * jax.nn.sigmoid(x)


def _split2(x):
    hi = x.astype(BF16)
    lo = (x - hi.astype(F32)).astype(BF16)
    return hi, lo


def _split3(x):
    hi = x.astype(BF16)
    r = x - hi.astype(F32)
    mid = r.astype(BF16)
    lo = (r - mid.astype(F32)).astype(BF16)
    return hi, mid, lo


def _dot(a, b):
    return jnp.dot(a, b, preferred_element_type=F32)


def _dot_nt(a, b):
    return lax.dot_general(a, b, (((1,), (1,)), ((), ())), preferred_element_type=F32)


def _dot_tn(a, b):
    return lax.dot_general(a, b, (((0,), (0,)), ((), ())), preferred_element_type=F32)


def _dot_exact_rhs(x, sel):
    hi, mid, lo = _split3(x)
    return _dot(hi, sel) + _dot(mid, sel) + _dot(lo, sel)


def _dot_exact_lhs(sel, x):
    hi, mid, lo = _split3(x)
    return _dot(sel, hi) + _dot(sel, mid) + _dot(sel, lo)


def _mod_kernel(c_ref, w_ref, b_ref, o_ref):
    a = _silu(c_ref[...]).astype(BF16)
    o_ref[0] = _dot(a, w_ref[0].astype(BF16)) + b_ref[0]


def _modulation(conds, ada_w, ada_b):
    nc = conds.shape[0]
    tn = 1024
    return pl.pallas_call(
        _mod_kernel,
        grid=(DEPTH, 3 * D_MODEL // tn),
        in_specs=[pl.BlockSpec((nc, D_MODEL), lambda l, n: (0, 0)),
                  pl.BlockSpec((1, D_MODEL, tn), lambda l, n: (l, 0, n)),
                  pl.BlockSpec((1, 1, tn), lambda l, n: (l, 0, n))],
        out_specs=pl.BlockSpec((1, nc, tn), lambda l, n: (l, 0, n)),
        out_shape=jax.ShapeDtypeStruct((DEPTH, nc, 3 * D_MODEL), F32),
        compiler_params=_cparams(("parallel", "parallel")),
        name="adaln_mod",
    )(conds, ada_w, ada_b.reshape(DEPTH, 1, 3 * D_MODEL))


def _addpe_kernel(x_ref, pe_ref, o_ref):
    o_ref[0] = x_ref[0] + pe_ref[...]


def _add_pos(x, pe, tm):
    B, T, D = x.shape
    return pl.pallas_call(
        _addpe_kernel,
        grid=(T // tm, B),
        in_specs=[pl.BlockSpec((1, tm, D), lambda i, b: (b, i, 0)),
                  pl.BlockSpec((tm, D), lambda i, b: (i, 0))],
        out_specs=pl.BlockSpec((1, tm, D), lambda i, b: (b, i, 0)),
        out_shape=jax.ShapeDtypeStruct(x.shape, F32),
        compiler_params=_cparams(("parallel", "parallel")),
        name="add_pos",
    )(x, pe)


def _inproj_kernel(x_ref, mod_ref, ng_ref, w_ref, pa_ref, pb_ref, pc_ref, pd_ref, ba_ref):
    x = x_ref[0]
    ms = jnp.mean(x * x, axis=-1, keepdims=True)
    y = x * lax.rsqrt(ms + EPS) * ng_ref[...]
    shift = mod_ref[0, :, 0:D_MODEL]
    scale = mod_ref[0, :, D_MODEL:2 * D_MODEL]
    h = (y * (1.0 + scale) + shift).astype(BF16)
    pa_ref[0] = _dot(h, w_ref[:, 0:1024]).astype(BF16)
    pb_ref[0] = _dot(h, w_ref[:, 1024:1792]).astype(BF16)
    pc_ref[0] = _dot(h, w_ref[:, 1792:2304]).astype(BF16)
    pd_ref[0] = _dot(h, w_ref[:, 2304:2816]).astype(BF16)
    ba_ref[0] = _dot(h, w_ref[:, 2816:2944])


def _inproj(x, mod, norm_g, w_in_p, tm):
    B, T, D = x.shape
    nmod = mod.shape[0]
    mod_map = (lambda b, i: (b, 0, 0)) if nmod == B else (lambda b, i: (0, 0, 0))
    tok = lambda n: pl.BlockSpec((1, tm, n), lambda b, i: (b, i, 0))
    return pl.pallas_call(
        _inproj_kernel,
        grid=(B, T // tm),
        in_specs=[tok(D),
                  pl.BlockSpec((1, 1, 3 * D), mod_map),
                  pl.BlockSpec((1, D), lambda b, i: (0, 0)),
                  pl.BlockSpec((D, P_PAD), lambda b, i: (0, 0))],
        out_specs=[tok(1024), tok(768), tok(512), tok(512), tok(128)],
        out_shape=[jax.ShapeDtypeStruct((B, T, 1024), BF16),
                   jax.ShapeDtypeStruct((B, T, 768), BF16),
                   jax.ShapeDtypeStruct((B, T, 512), BF16),
                   jax.ShapeDtypeStruct((B, T, 512), BF16),
                   jax.ShapeDtypeStruct((B, T, 128), F32)],
        compiler_params=_cparams(("parallel", "parallel")),
        name="in_proj",
    )(x, mod, norm_g.reshape(1, D), w_in_p)


def _outproj_body(ya_ref, yb_ref, yc_ref, yd_ref, x_ref, mod_ref, w_ref):
    ycat = jnp.concatenate([ya_ref[0], yb_ref[0], yc_ref[0], yd_ref[0]], axis=-1)
    out = _dot(ycat, w_ref[...])
    gate = mod_ref[0, :, 2 * D_MODEL:3 * D_MODEL]
    return x_ref[0] + gate * out


def _outproj_kernel(ya_ref, yb_ref, yc_ref, yd_ref, x_ref, mod_ref, w_ref, o_ref):
    o_ref[0] = _outproj_body(ya_ref, yb_ref, yc_ref, yd_ref, x_ref, mod_ref, w_ref)


def _outproj_final_kernel(ya_ref, yb_ref, yc_ref, yd_ref, x_ref, mod_ref, w_ref, fg_ref, o_ref):
    xn = _outproj_body(ya_ref, yb_ref, yc_ref, yd_ref, x_ref, mod_ref, w_ref)
    ms = jnp.mean(xn * xn, axis=-1, keepdims=True)
    o_ref[0] = xn * lax.rsqrt(ms + EPS) * fg_ref[...]


def _outproj(ya, yb, yc, yd, x, mod, w_out, final_g, tm):
    B, T, D = x.shape
    nmod = mod.shape[0]
    mod_map = (lambda b, i: (b, 0, 0)) if nmod == B else (lambda b, i: (0, 0, 0))
    tok = lambda n: pl.BlockSpec((1, tm, n), lambda b, i: (b, i, 0))
    in_specs = [tok(256), tok(256), tok(256), tok(256), tok(D),
                pl.BlockSpec((1, 1, 3 * D), mod_map),
                pl.BlockSpec((D, D), lambda b, i: (0, 0))]
    args = [ya, yb, yc, yd, x, mod, w_out]
    kern = _outproj_kernel
    if final_g is not None:
        in_specs.append(pl.BlockSpec((1, D), lambda b, i: (0, 0)))
        args.append(final_g.reshape(1, D))
        kern = _outproj_final_kernel
    return pl.pallas_call(
        kern,
        grid=(B, T // tm),
        in_specs=in_specs,
        out_specs=tok(D),
        out_shape=jax.ShapeDtypeStruct((B, T, D), F32),
        compiler_params=_cparams(("parallel", "parallel")),
        name="out_proj",
    )(*args)


def _gelu_tanh(x):
    return 0.5 * x * (1.0 + jnp.tanh(math.sqrt(2.0 / math.pi) * (x + 0.044715 * (x * x * x))))


def _diag_blocks(r, lane_grp):
    n = r.shape[0] // 4
    out = r[0:n]
    for g in range(1, 4):
        out = jnp.where(lane_grp == g, r[g * n:(g + 1) * n], out)
    return out


def _local_kernel(pb_ref, pc_ref, prev_ref, next_ref, sgw_ref, sgb_ref, sgn_ref, band_ref, pw_ref, ps_ref,
                  yb_ref, yc_ref, *, T, tl):
    i = pl.program_id(1)
    nblk = T // tl
    n = SGU_CHUNK
    lane_grp = lax.broadcasted_iota(jnp.int32, (n, 256), 1) // 64
    row = lax.broadcasted_iota(jnp.int32, (n, 256), 0)
    lo_w = jnp.zeros((n, 256), jnp.int32)
    hi_w = jnp.zeros((n, 256), jnp.int32)
    for g, w in enumerate(POOL_WINDOWS):
        lo_w = jnp.where(lane_grp == g, w // 2, lo_w)
        hi_w = jnp.where(lane_grp == g, w - w // 2, hi_w)
    prev_blk = jnp.where(i > 0, prev_ref[0, :, 0:256], jnp.zeros((HALO, 256), BF16))
    next_blk = jnp.where(i < nblk - 1, next_ref[0, :, 0:256], jnp.zeros((HALO, 256), BF16))
    nch = tl // n
    for j in range(nch):
        r0 = j * n
        uvg = pb_ref[0, r0:r0 + n, :]
        gel = _gelu_tanh(uvg[:, 0:512].astype(F32))
        u = gel[:, 0:256]
        v = gel[:, 256:512]
        vn = v * lax.rsqrt(jnp.mean(v * v, axis=-1, keepdims=True) + EPS) * sgn_ref[...]
        sv = _diag_blocks(_dot(sgw_ref[...], vn.astype(BF16)), lane_grp) + sgb_ref[...]
        yb_ref[0, r0:r0 + n, :] = (u * sv * _silu(uvg[:, 512:768].astype(F32))).astype(BF16)
        xt = pc_ref[0, r0:r0 + n, 0:256]
        pv = prev_blk if j == 0 else pc_ref[0, r0 - HALO:r0, 0:256]
        nx = next_blk if j == nch - 1 else pc_ref[0, r0 + n:r0 + n + HALO, 0:256]
        xcat = jnp.concatenate([pv, xt, nx], axis=0)
        wsum = _diag_blocks(_dot(band_ref[...], xcat), lane_grp)
        t = i * tl + r0 + row
        cnt = (jnp.minimum(t + hi_w, T) - jnp.maximum(t - lo_w, 0)).astype(F32)
        pooled = wsum / cnt - xt.astype(F32)
        yc = _dot(pooled.astype(BF16), pw_ref[...]) * ps_ref[...]
        yc_ref[0, r0:r0 + n, :] = (yc * _silu(pc_ref[0, r0:r0 + n, 256:512].astype(F32))).astype(BF16)


def _local(pb, pc, sgw, sgb, sgn, band, pw_bd, pscale, tl):
    B, T, _ = pb.shape
    hb = tl // HALO
    nh = T // HALO
    const = lambda shape: pl.BlockSpec(shape, lambda b, i: (0,) * len(shape))
    return pl.pallas_call(
        functools.partial(_local_kernel, T=T, tl=tl),
        grid=(B, T // tl),
        in_specs=[pl.BlockSpec((1, tl, 768), lambda b, i: (b, i, 0)),
                  pl.BlockSpec((1, tl, 512), lambda b, i: (b, i, 0)),
                  pl.BlockSpec((1, HALO, 512), lambda b, i: (b, jnp.maximum(i * hb - 1, 0), 0)),
                  pl.BlockSpec((1, HALO, 512), lambda b, i: (b, jnp.minimum((i + 1) * hb, nh - 1), 0)),
                  const((4 * SGU_CHUNK, SGU_CHUNK)), const((SGU_CHUNK, 256)), const((1, 256)),
                  const((4 * SGU_CHUNK, SGU_CHUNK + 2 * HALO)), const((256, 256)), const((1, 256))],
        out_specs=[pl.BlockSpec((1, tl, 256), lambda b, i: (b, i, 0)),
                   pl.BlockSpec((1, tl, 256), lambda b, i: (b, i, 0))],
        out_shape=[jax.ShapeDtypeStruct((B, T, 256), BF16), jax.ShapeDtypeStruct((B, T, 256), BF16)],
        compiler_params=_cparams(("parallel", "parallel")),
        name="sgu_pool",
    )(pb, pc, pc, pc, sgw, sgb, sgn, band, pw_bd, pscale)


def _dft1_kernel(pd_ref, cs_ref, z_ref):
    z = _dot(pd_ref[0, :, 0:256], cs_ref[...])
    z_ref[0, 0] = z[:, 0:256].astype(BF16)
    z_ref[0, 1] = z[:, 256:512].astype(BF16)


def _dft2_kernel(f_ref, z_ref, pd_ref, fw_ref, y_ref, *, inv_norm):
    f = _dot(f_ref[...], z_ref[0]) * inv_norm
    y = _dot(f.astype(BF16), fw_ref[...])
    y_ref[0] = (y * _silu(pd_ref[0, :, 256:512].astype(F32))).astype(BF16)


def _fourier(pd, cs, fmat, fw_bd, tm, tq):
    B, T, _ = pd.shape
    z = pl.pallas_call(
        _dft1_kernel,
        grid=(B, T // tm),
        in_specs=[pl.BlockSpec((1, tm, 512), lambda b, i: (b, i, 0)),
                  pl.BlockSpec((256, 512), lambda b, i: (0, 0))],
        out_specs=pl.BlockSpec((1, 2, tm, 256), lambda b, i: (b, 0, i, 0)),
        out_shape=jax.ShapeDtypeStruct((B, 2, T, 256), BF16),
        compiler_params=_cparams(("parallel", "parallel")),
        name="dft_channels",
    )(pd, cs)
    z = z.reshape(B, 2 * T, 256)
    return pl.pallas_call(
        functools.partial(_dft2_kernel, inv_norm=1.0 / math.sqrt(T * C_D)),
        grid=(T // tq, B),
        in_specs=[pl.BlockSpec((tq, 2 * T), lambda i, b: (i, 0)),
                  pl.BlockSpec((1, 2 * T, 256), lambda i, b: (b, 0, 0)),
                  pl.BlockSpec((1, tq, 512), lambda i, b: (b, i, 0)),
                  pl.BlockSpec((256, 256), lambda i, b: (0, 0))],
        out_specs=pl.BlockSpec((1, tq, 256), lambda i, b: (b, i, 0)),
        out_shape=jax.ShapeDtypeStruct((B, T, 256), BF16),
        compiler_params=_cparams(("parallel", "parallel")),
        name="dft_positions",
    )(fmat, z, pd, fw_bd)


def _bd_mask():
    r = lax.broadcasted_iota(jnp.int32, (256, 256), 0)
    c = lax.broadcasted_iota(jnp.int32, (256, 256), 1)
    return (r // 64) == (c // 64), r % 64, c % 64


def _expand(x, bd):
    return jnp.where(bd, jnp.concatenate([x, x, x, x], axis=0), jnp.zeros((), x.dtype))


def _compress(x):
    return x[0:64] + x[64:128] + x[128:192] + x[192:256]


def _delta_prep_kernel(qkv_ref, prev_ref, next_ref, ba_ref, cw_ref, par_ref, sel_ref, grp_ref,
                       d0_ref, d1_ref, a_ref, gl_ref, *, T, tc):
    j = pl.program_id(1)
    nblk = T // tc
    x = qkv_ref[0].astype(F32)
    prev = jnp.where(j > 0, prev_ref[0].astype(F32), 0.0)
    nxt = jnp.where(j < nblk - 1, next_ref[0].astype(F32), 0.0)
    xcat = jnp.concatenate([prev, x, nxt], axis=0)
    n = tc + 2 * HALO
    acc = None
    for tap in range(CONV_W):
        s = (CONV_W // 2 - tap) % n
        xs = xcat if s == 0 else pltpu.roll(xcat, s, axis=0)
        term = xs[HALO:HALO + tc] * cw_ref[tap:tap + 1, :]
        acc = term if acc is None else acc + term
    c = _silu(acc)
    cq, ck, v = c[:, 0:256], c[:, 256:512], c[:, 512:768]
    grp = grp_ref[...]
    qh, ql = _split2(cq * cq)
    kh, kl = _split2(ck * ck)
    q = cq * lax.rsqrt(_dot(qh, grp) + _dot(ql, grp) + EPS) * (DK ** -0.5)
    k = ck * lax.rsqrt(_dot(kh, grp) + _dot(kl, grp) + EPS)
    ba = ba_ref[0]
    z = ba + par_ref[1:2, :]
    softplus = jnp.maximum(z, 0.0) + jnp.log1p(jnp.exp(-jnp.abs(z)))
    lane = lax.broadcasted_iota(jnp.int32, ba.shape, 1)
    bg = jnp.where(lane < 2 * H_A, jax.nn.sigmoid(ba), -jnp.exp(par_ref[0:1, :]) * softplus)
    bgx = _dot_exact_rhs(bg, sel_ref[...])
    rr = lax.broadcasted_iota(jnp.int32, (tc, tc), 0)
    cc = lax.broadcasted_iota(jnp.int32, (tc, tc), 1)
    same = (rr // DN_CHUNK) == (cc // DN_CHUNK)
    tri_f = jnp.where(same & (rr >= cc), 1.0, 0.0).astype(BF16)
    tri_b = jnp.where(same & (rr <= cc), 1.0, 0.0).astype(BF16)
    gcum = (_dot_exact_lhs(tri_f, bgx[:, 512:768]), _dot_exact_lhs(tri_b, bgx[:, 768:1024]))

    bd, ri, ci = _bd_mask()
    ones_col = jnp.ones((256, DN_CHUNK), BF16)
    r64 = lax.broadcasted_iota(jnp.int32, (DN_CHUNK, 256), 0)
    c64 = lax.broadcasted_iota(jnp.int32, (DN_CHUNK, 256), 1) % 64
    diag64 = r64 == c64
    out_refs = (d0_ref, d1_ref)
    for ch in range(tc // DN_CHUNK):
        r0 = ch * DN_CHUNK
        kx = k[r0:r0 + DN_CHUNK]
        qx = q[r0:r0 + DN_CHUNK]
        vx = v[r0:r0 + DN_CHUNK]
        k_exp = _expand(kx.astype(BF16), bd)
        q_exp = _expand(qx.astype(BF16), bd)
        kq = _dot_nt(jnp.concatenate([k_exp, q_exp], axis=0), k_exp)
        kk, qk = kq[0:256], kq[256:512]
        for d in range(2):
            gc = gcum[d][r0:r0 + DN_CHUNK]
            beta = bgx[r0:r0 + DN_CHUNK, d * 256:(d + 1) * 256]
            col = jnp.concatenate([gc, gc, gc, gc], axis=0)
            rowv = _dot_exact_lhs(ones_col, jnp.where(diag64, gc, 0.0))
            incl = bd & ((ri >= ci) if d == 0 else (ri <= ci))
            strict = bd & ((ri > ci) if d == 0 else (ri < ci))
            decay = jnp.where(incl, jnp.exp(jnp.where(incl, col - rowv, 0.0)), 0.0)
            bcol = jnp.concatenate([beta, beta, beta, beta], axis=0)
            a_c = _compress(jnp.where(strict, bcol * kk * decay, 0.0))
            a_ref[d, 0, r0:r0 + DN_CHUNK, :] = a_c[:, 0:128]
            a_ref[d, 1, r0:r0 + DN_CHUNK, :] = a_c[:, 128:256]
            egc = jnp.exp(gc)
            g_last = gc[DN_CHUNK - 1:DN_CHUNK] if d == 0 else gc[0:1]
            o = out_refs[d]
            o[0, r0:r0 + DN_CHUNK, 0:256] = (vx * beta).astype(BF16)
            o[0, r0:r0 + DN_CHUNK, 256:512] = (kx * beta * egc).astype(BF16)
            o[0, r0:r0 + DN_CHUNK, 512:768] = (qx * egc).astype(BF16)
            o[0, r0:r0 + DN_CHUNK, 768:1024] = (kx * jnp.exp(g_last - gc)).astype(BF16)
            o[0, r0:r0 + DN_CHUNK, 1024:1280] = _compress(qk * decay).astype(BF16)
            gl_ref[0, d, ch] = jnp.exp(g_last)


def _delta_prep(pa, ba, conv_w, par, sel, grp, tc):
    B, T, _ = pa.shape
    hb = tc // HALO
    nh = T // HALO
    nt = T // tc
    nch = tc // DN_CHUNK
    const = lambda shape: pl.BlockSpec(shape, lambda b, i: (0,) * len(shape))
    return pl.pallas_call(
        functools.partial(_delta_prep_kernel, T=T, tc=tc),
        grid=(B, nt),
        in_specs=[pl.BlockSpec((1, tc, 768), lambda b, i: (b, i, 0)),
                  pl.BlockSpec((1, HALO, 768), lambda b, i: (b, jnp.maximum(i * hb - 1, 0), 0)),
                  pl.BlockSpec((1, HALO, 768), lambda b, i: (b, jnp.minimum((i + 1) * hb, nh - 1), 0)),
                  pl.BlockSpec((1, tc, 128), lambda b, i: (b, i, 0)),
                  const((CONV_W, 768)), const((2, 128)), const((128, 1024)), const((256, 256))],
        out_specs=[pl.BlockSpec((1, tc, DREC), lambda b, i: (b, i, 0)),
                   pl.BlockSpec((1, tc, DREC), lambda b, i: (b, i, 0)),
                   pl.BlockSpec((2, 2, tc, 128), lambda b, i: (0, 0, b * nt + i, 0)),
                   pl.BlockSpec((1, 2, nch, 1, 256), lambda b, i: (b, 0, i, 0, 0))],
        out_shape=[jax.ShapeDtypeStruct((B, T, DREC), BF16),
                   jax.ShapeDtypeStruct((B, T, DREC), BF16),
                   jax.ShapeDtypeStruct((2, 2, B * T, 128), F32),
                   jax.ShapeDtypeStruct((B, 2, T // DN_CHUNK, 1, 256), F32)],
        compiler_params=_cparams(("parallel", "parallel")),
        name="delta_prep",
    )(pa, pa, pa, ba, conv_w, par, sel, grp)


TRI_UNITS = 128
TRI_ROWS = TRI_UNITS * DN_CHUNK


def _tri_rows(x_ref, lower):
    sub = lax.broadcasted_iota(jnp.int32, (8, TRI_UNITS), 0)

    def row(step, carry):
        i = step if lower else DN_CHUNK - 1 - step
        accs = [[jnp.zeros((8, TRI_UNITS), F32) for _ in range(8)] for _ in range(H_A)]
        for jb in range(8):
            cvs = range(jb + 1) if lower else range(jb, 8)
            if lower:
                j0 = 8 * jb
                trips = jnp.clip(i - j0, 0, 8)
            else:
                j0 = jnp.maximum(8 * jb, i + 1)
                trips = jnp.clip(8 * jb + 8 - j0, 0, 8)

            def inner(jj, acc, j0=j0, cvs=cvs):
                j = j0 + jj
                out = [list(a) for a in acc]
                for h in range(H_A):
                    coef = jnp.broadcast_to(x_ref[i, pl.ds(h * DK + j, 1), :], (8, TRI_UNITS))
                    for cv in cvs:
                        out[h][cv] = out[h][cv] + coef * x_ref[j, h * DK + 8 * cv:h * DK + 8 * cv + 8, :]
                return out

            accs = lax.fori_loop(0, trips, inner, accs)
        for h in range(H_A):
            for cv in range(8):
                x_ref[i, h * DK + 8 * cv:h * DK + 8 * cv + 8, :] = jnp.where(sub + 8 * cv == i, 1.0, -accs[h][cv])
        return carry

    lax.fori_loop(0, DN_CHUNK, row, 0)


def _tri_inverse_kernel(a_ref, t_ref, x_ref):
    d = pl.program_id(0)

    def load(i, carry):
        for half in range(2):
            x_ref[i, 128 * half:128 * half + 128, :] = a_ref[0, half, pl.ds(i, TRI_UNITS, stride=DN_CHUNK), :].T
        return carry

    lax.fori_loop(0, DN_CHUNK, load, 0)

    @pl.when(d == 0)
    def _():
        _tri_rows(x_ref, True)

    @pl.when(d == 1)
    def _():
        _tri_rows(x_ref, False)

    def store(i, carry):
        for half in range(2):
            t_ref[0, half, pl.ds(i, TRI_UNITS, stride=DN_CHUNK), :] = x_ref[i, 128 * half:128 * half + 128, :].T
        return carry

    lax.fori_loop(0, DN_CHUNK, store, 0)


def _tri_inverse(a):
    rows = a.shape[2]
    spec = pl.BlockSpec((1, 2, TRI_ROWS, 128), lambda d, i: (d, 0, i, 0))
    return pl.pallas_call(
        _tri_inverse_kernel,
        grid=(2, rows // TRI_ROWS),
        in_specs=[spec],
        out_specs=spec,
        out_shape=jax.ShapeDtypeStruct(a.shape, F32),
        scratch_shapes=[pltpu.VMEM((DN_CHUNK, 256, TRI_UNITS), F32)],
        compiler_params=_cparams(("parallel", "parallel")),
        name="tri_inverse",
    )(a)


def _delta_scan_kernel(d0_ref, d1_ref, tf_ref, tb_ref, glf_ref, glb_ref, s0_ref, of_ref, ob_ref, sfin_ref,
                       sf_ref, sb_ref, *, tc):
    j = pl.program_id(1)
    nblk = pl.num_programs(1)
    nch = tc // DN_CHUNK

    @pl.when(j == 0)
    def _():
        sf_ref[...] = s0_ref[0, 0]
        sb_ref[...] = s0_ref[0, 1]

    bd, _, _ = _bd_mask()

    def step(rec_ref, t_ref, gl, s_ref, o_ref, r0):
        rows = slice(r0, r0 + DN_CHUNK)
        tinv = jnp.concatenate([t_ref[0, 0, rows, :], t_ref[0, 1, rows, :]], axis=1)
        tinv = _expand(tinv.astype(BF16), bd)
        rhs = jnp.concatenate([_expand(rec_ref[0, rows, 0:256], bd), _expand(rec_ref[0, rows, 256:512], bd)], axis=1)
        uw = _dot(tinv, rhs)
        qd = _expand(rec_ref[0, rows, 512:768], bd)
        kd = _expand(rec_ref[0, rows, 768:1024], bd)
        qkd = _expand(rec_ref[0, rows, 1024:1280], bd)
        s = s_ref[...]
        ws_qs = _dot(jnp.concatenate([uw[:, 256:512].astype(BF16), qd], axis=0), s.astype(BF16))
        v_new = (uw[:, 0:256] - ws_qs[0:256]).astype(BF16)
        o = ws_qs[256:512] + _dot(qkd, v_new)
        o_ref[0, rows, :] = _compress(o)
        s_ref[...] = s * gl + _dot_tn(kd, v_new)

    for ch in range(nch):
        step(d0_ref, tf_ref, glf_ref[0, 0, ch], sf_ref, of_ref, ch * DN_CHUNK)
        cb = nch - 1 - ch
        step(d1_ref, tb_ref, glb_ref[0, 0, cb], sb_ref, ob_ref, cb * DN_CHUNK)

    @pl.when(j == nblk - 1)
    def _():
        sfin_ref[0, 0] = sf_ref[...]
        sfin_ref[0, 1] = sb_ref[...]


def _delta_scan(d0, d1, tinv, gl, s0_bd, tc):
    B, T, _ = d0.shape
    nblk = T // tc
    nch = tc // DN_CHUNK
    return pl.pallas_call(
        functools.partial(_delta_scan_kernel, tc=tc),
        grid=(B, nblk),
        in_specs=[pl.BlockSpec((1, tc, DREC), lambda b, j: (b, j, 0)),
                  pl.BlockSpec((1, tc, DREC), lambda b, j: (b, nblk - 1 - j, 0)),
                  pl.BlockSpec((1, 2, tc, 128), lambda b, j: (0, 0, b * nblk + j, 0)),
                  pl.BlockSpec((1, 2, tc, 128), lambda b, j: (1, 0, b * nblk + nblk - 1 - j, 0)),
                  pl.BlockSpec((1, 1, nch, 1, 256), lambda b, j: (b, 0, j, 0, 0)),
                  pl.BlockSpec((1, 1, nch, 1, 256), lambda b, j: (b, 1, nblk - 1 - j, 0, 0)),
                  pl.BlockSpec((1, 2, 256, 256), lambda b, j: (b, 0, 0, 0))],
        out_specs=[pl.BlockSpec((1, tc, 256), lambda b, j: (b, j, 0)),
                   pl.BlockSpec((1, tc, 256), lambda b, j: (b, nblk - 1 - j, 0)),
                   pl.BlockSpec((1, 2, 256, 256), lambda b, j: (b, 0, 0, 0))],
        out_shape=[jax.ShapeDtypeStruct((B, T, 256), F32),
                   jax.ShapeDtypeStruct((B, T, 256), F32),
                   jax.ShapeDtypeStruct((B, 2, 256, 256), F32)],
        scratch_shapes=[pltpu.VMEM((256, 256), F32), pltpu.VMEM((256, 256), F32)],
        compiler_params=_cparams(("parallel", "arbitrary")),
        name="delta_scan",
    )(d0, d1, tinv, tinv, gl, gl, s0_bd)


def _delta_out_kernel(of_ref, ob_ref, ga_ref, grp_ref, ng_ref, y_ref):
    o = of_ref[0] + ob_ref[0]
    hi, lo = _split2(o * o)
    ms = (_dot(hi, grp_ref[...]) + _dot(lo, grp_ref[...])) * (1.0 / DK)
    y = o * lax.rsqrt(ms + EPS) * ng_ref[...]
    y_ref[0] = (y * _silu(ga_ref[0].astype(F32))).astype(BF16)


def _delta_out(o_f, o_b, pa, grp, ng4, tm):
    B, T, _ = o_f.shape
    tok = pl.BlockSpec((1, tm, 256), lambda b, i: (b, i, 0))
    return pl.pallas_call(
        _delta_out_kernel,
        grid=(B, T // tm),
        in_specs=[tok, tok,
                  pl.BlockSpec((1, tm, 256), lambda b, i: (b, i, 3)),
                  pl.BlockSpec((256, 256), lambda b, i: (0, 0)),
                  pl.BlockSpec((1, 256), lambda b, i: (0, 0))],
        out_specs=tok,
        out_shape=jax.ShapeDtypeStruct((B, T, 256), BF16),
        compiler_params=_cparams(("parallel", "parallel")),
        name="delta_out",
    )(o_f, o_b, pa, grp, ng4)


def _block_diag4(w):
    n = w.shape[-1]
    eye = jnp.eye(4, dtype=w.dtype)
    return jnp.einsum("gij,gh->gihj", w, eye).reshape(4 * n, 4 * n)


def _dft_matrix(T):
    t = jnp.arange(T, dtype=jnp.int32)
    ang = ((t[:, None] * t[None, :]) % T).astype(F32) * (2.0 * math.pi / T)
    return jnp.concatenate([jnp.cos(ang), jnp.sin(ang)], axis=1).astype(BF16)


def _channel_dft():
    c = jnp.arange(C_D, dtype=jnp.int32)
    ang = ((c[:, None] * c[None, :]) % C_D).astype(F32) * (2.0 * math.pi / C_D)
    cos4 = _block_diag4(jnp.broadcast_to(jnp.cos(ang), (G_D, C_D, C_D)))
    sin4 = _block_diag4(jnp.broadcast_to(jnp.sin(ang), (G_D, C_D, C_D)))
    return jnp.concatenate([cos4, -sin4], axis=1).astype(BF16)


def _pool_band():
    n = SGU_CHUNK
    t = jnp.arange(n)[:, None]
    s = jnp.arange(n + 2 * HALO)[None, :] - HALO
    blocks = [((s >= t - w // 2) & (s < t + w - w // 2)) for w in POOL_WINDOWS]
    return jnp.concatenate(blocks, axis=0).astype(BF16)


def _gate_selector():
    src = jnp.arange(128)[:, None]
    dst = jnp.arange(1024)[None, :]
    kind, d, h = dst // 512, (dst // 256) % 2, (dst // 64) % 4
    return (src == kind * 8 + d * 4 + h).astype(BF16)


def _grid_pos_embed(T):
    rows = T // GRID_W
    r = jnp.repeat(jnp.arange(rows, dtype=F32), GRID_W)
    col = jnp.tile(jnp.arange(GRID_W, dtype=F32), rows)
    n_freq = D_MODEL // 4
    freqs = jnp.power(POS_BASE, -jnp.arange(n_freq, dtype=F32) / n_freq)
    ar = r[:, None] * freqs[None]
    ac = col[:, None] * freqs[None]
    return jnp.concatenate([jnp.sin(ar), jnp.cos(ar), jnp.sin(ac), jnp.cos(ac)], axis=-1)


def _layer_consts(l, w):
    win = w["w_in"][l]
    w_in_p = jnp.concatenate([win[:, 0:1024], win[:, 1040:2832], win[:, 1024:1040],
                              jnp.zeros((D_MODEL, P_PAD - 2832), F32)], axis=1).astype(BF16)
    par = jnp.zeros((2, 128), F32)
    par = par.at[0, 8:16].set(w["a_log"][l].reshape(8)).at[1, 8:16].set(w["dt_bias"][l].reshape(8))
    return dict(
        norm_g=w["norm_g"][l], w_in_p=w_in_p, conv_w=w["conv_qkv"][l], par=par,
        dn_g4=jnp.tile(w["dn_norm_g"][l], H_A).reshape(1, 256),
        sgw=w["sgu_w"][l].reshape(G_B * SGU_CHUNK, SGU_CHUNK).astype(BF16),
        sgb=jnp.repeat(w["sgu_b"][l].T, W_B // G_B, axis=1),
        sgn=w["sgu_norm_g"][l].reshape(1, W_B),
        pw_bd=_block_diag4(w["pool_w"][l]).astype(BF16),
        pscale=w["pool_scale"][l].reshape(1, W_C),
        fw_bd=_block_diag4(w["fourier_w"][l]).astype(BF16),
        w_out=w["w_out"][l].astype(BF16),
    )


def _trunk_layer(x, mod, s0_bd, lc, shared, fmat, final_g):
    B, T, _ = x.shape
    tm = min(T, 512)
    pa, pb, pc, pd, ba = _inproj(x, mod, lc["norm_g"], lc["w_in_p"], tm)
    d0, d1, a_tri, gl = _delta_prep(pa, ba, lc["conv_w"], lc["par"], shared["sel"], shared["grp"], 256)
    o_f, o_b, s_fin = _delta_scan(d0, d1, _tri_inverse(a_tri), gl, s0_bd, 256)
    ya = _delta_out(o_f, o_b, pa, shared["grp"], lc["dn_g4"], tm)
    yb, yc = _local(pb, pc, lc["sgw"], lc["sgb"], lc["sgn"], shared["band"], lc["pw_bd"], lc["pscale"], tm)
    yd = _fourier(pd, shared["cs"], fmat, lc["fw_bd"], tm, tm)
    xn = _outproj(ya, yb, yc, yd, x, mod, lc["w_out"], final_g, tm)
    return xn, s_fin


def _expand_state(s):
    B = s.shape[0]
    eye = jnp.eye(H_A, dtype=s.dtype)
    return jnp.einsum("bdhij,hg->bdhigj", s, eye).reshape(B, 2, H_A * DK, H_A * DK)


def _extract_state(s_bd):
    B = s_bd.shape[0]
    s = s_bd.reshape(B, 2, H_A, DK, H_A, DK)
    return jnp.stack([s[:, :, h, :, h, :] for h in range(H_A)], axis=2)


def kernel(x_prompt, x_sample, state_delta, c, c_ctx, ada_w, ada_b, norm_g, w_in, conv_qkv, a_log, dt_bias,
           dn_norm_g, sgu_norm_g, sgu_w, sgu_b, pool_w, pool_scale, fourier_w, w_out, final_norm_g):
    w = dict(norm_g=norm_g, w_in=w_in, conv_qkv=conv_qkv, a_log=a_log, dt_bias=dt_bias, dn_norm_g=dn_norm_g,
             sgu_norm_g=sgu_norm_g, sgu_w=sgu_w, sgu_b=sgu_b, pool_w=pool_w, pool_scale=pool_scale,
             fourier_w=fourier_w, w_out=w_out)
    bp, tp, _ = x_prompt.shape
    bs, ts, _ = x_sample.shape
    conds = jnp.concatenate([c, c_ctx[None], jnp.zeros((16 - bs - 1, D_MODEL), F32)], axis=0)
    mod = _modulation(conds, ada_w, ada_b)
    ones64 = jnp.ones((H_A, DK, DK), F32)
    shared = dict(sel=_gate_selector(), grp=_block_diag4(ones64).astype(BF16), band=_pool_band(),
                  cs=_channel_dft())
    fmat_p = _dft_matrix(tp)
    fmat_s = _dft_matrix(ts)
    layers = [_layer_consts(l, w) for l in range(DEPTH)]

    xp = x_prompt
    xs = _add_pos(x_sample, _grid_pos_embed(ts), min(ts, 512))
    zero_state = jnp.zeros((bp, 2, H_A * DK, H_A * DK), F32)
    ctx_states = []
    for l in range(DEPTH):
        fg = final_norm_g if l == DEPTH - 1 else None
        xp, s_l = _trunk_layer(xp, mod[l, bs:bs + 1].reshape(1, 1, 3 * D_MODEL), zero_state, layers[l], shared,
                               fmat_p, fg)
        ctx_states.append(_extract_state(s_l))
        xs, _ = _trunk_layer(xs, mod[l, 0:bs].reshape(bs, 1, 3 * D_MODEL), _expand_state(state_delta[:, l]),
                             layers[l], shared, fmat_s, fg)
    return xp, xs, jnp.stack(ctx_states, axis=1)
```

```python
import functools
import math

import numpy as np
import jax
import jax.numpy as jnp
from jax import lax
from jax.experimental import pallas as pl
from jax.experimental.pallas import tpu as pltpu

F32 = jnp.float32
BF16 = jnp.bfloat16

D_MODEL = 1024
DEPTH = 4
GRID_W = 64
POS_BASE = 10000.0
H_A = 4
DK = 64
W_A = 256
CONV_W = 4
DN_CHUNK = 64
W_B = 256
G_B = 4
SGU_CHUNK = 128
W_C = 256
POOL_WINDOWS = (2, 4, 8, 16)
W_D = 256
G_D = 4
C_D = 64
EPS = 1e-6

P_MAIN = 2816
P_PAD = P_MAIN + 128
HALO = 16
DREC = 5 * W_A

VMEM_LIMIT = 56 * 1024 * 1024


def _cparams(sem):
    return pltpu.CompilerParams(dimension_semantics=sem, vmem_limit_bytes=VMEM_LIMIT)


def _silu(x):
    return x * jax.nn.sigmoid(x)


def _split2(x):
    hi = x.astype(BF16)
    lo = (x - hi.astype(F32)).astype(BF16)
    return hi, lo


def _split3(x):
    hi = x.astype(BF16)
    r = x - hi.astype(F32)
    mid = r.astype(BF16)
    lo = (r - mid.astype(F32)).astype(BF16)
    return hi, mid, lo


def _dot(a, b):
    return jnp.dot(a, b, preferred_element_type=F32)


def _dot_nt(a, b):
    return lax.dot_general(a, b, (((1,), (1,)), ((), ())), preferred_element_type=F32)


def _dot_tn(a, b):
    return lax.dot_general(a, b, (((0,), (0,)), ((), ())), preferred_element_type=F32)


def _dot_exact_rhs(x, sel):
    hi, mid, lo = _split3(x)
    return _dot(hi, sel) + _dot(mid, sel) + _dot(lo, sel)


def _dot_exact_lhs(sel, x):
    hi, mid, lo = _split3(x)
    return _dot(sel, hi) + _dot(sel, mid) + _dot(sel, lo)


def _mod_kernel(c_ref, w_ref, b_ref, o_ref):
    a = _silu(c_ref[...]).astype(BF16)
    o_ref[0] = _dot(a, w_ref[0].astype(BF16)) + b_ref[0]


def _modulation(conds, ada_w, ada_b):
    nc = conds.shape[0]
    tn = 1024
    return pl.pallas_call(
        _mod_kernel,
        grid=(DEPTH, 3 * D_MODEL // tn),
        in_specs=[pl.BlockSpec((nc, D_MODEL), lambda l, n: (0, 0)),
                  pl.BlockSpec((1, D_MODEL, tn), lambda l, n: (l, 0, n)),
                  pl.BlockSpec((1, 1, tn), lambda l, n: (l, 0, n))],
        out_specs=pl.BlockSpec((1, nc, tn), lambda l, n: (l, 0, n)),
        out_shape=jax.ShapeDtypeStruct((DEPTH, nc, 3 * D_MODEL), F32),
        compiler_params=_cparams(("parallel", "parallel")),
        name="adaln_mod",
    )(conds, ada_w, ada_b.reshape(DEPTH, 1, 3 * D_MODEL))


def _addpe_kernel(x_ref, pe_ref, o_ref):
    o_ref[0] = x_ref[0] + pe_ref[...]


def _add_pos(x, pe, tm):
    B, T, D = x.shape
    return pl.pallas_call(
        _addpe_kernel,
        grid=(T // tm, B),
        in_specs=[pl.BlockSpec((1, tm, D), lambda i, b: (b, i, 0)),
                  pl.BlockSpec((tm, D), lambda i, b: (i, 0))],
        out_specs=pl.BlockSpec((1, tm, D), lambda i, b: (b, i, 0)),
        out_shape=jax.ShapeDtypeStruct(x.shape, F32),
        compiler_params=_cparams(("parallel", "parallel")),
        name="add_pos",
    )(x, pe)


def _inproj_kernel(x_ref, mod_ref, ng_ref, w_ref, pa_ref, pb_ref, pc_ref, pd_ref, ba_ref):
    x = x_ref[0]
    ms = jnp.mean(x * x, axis=-1, keepdims=True)
    y = x * lax.rsqrt(ms + EPS) * ng_ref[...]
    shift = mod_ref[0, :, 0:D_MODEL]
    scale = mod_ref[0, :, D_MODEL:2 * D_MODEL]
    h = (y * (1.0 + scale) + shift).astype(BF16)
    pa_ref[0] = _dot(h, w_ref[:, 0:1024]).astype(BF16)
    pb_ref[0] = _dot(h, w_ref[:, 1024:1792]).astype(BF16)
    pc_ref[0] = _dot(h, w_ref[:, 1792:2304]).astype(BF16)
    pd_ref[0] = _dot(h, w_ref[:, 2304:2816]).astype(BF16)
    ba_ref[0] = _dot(h, w_ref[:, 2816:2944])


def _inproj(x, mod, norm_g, w_in_p, tm):
    B, T, D = x.shape
    nmod = mod.shape[0]
    mod_map = (lambda b, i: (b, 0, 0)) if nmod == B else (lambda b, i: (0, 0, 0))
    tok = lambda n: pl.BlockSpec((1, tm, n), lambda b, i: (b, i, 0))
    return pl.pallas_call(
        _inproj_kernel,
        grid=(B, T // tm),
        in_specs=[tok(D),
                  pl.BlockSpec((1, 1, 3 * D), mod_map),
                  pl.BlockSpec((1, D), lambda b, i: (0, 0)),
                  pl.BlockSpec((D, P_PAD), lambda b, i: (0, 0))],
        out_specs=[tok(1024), tok(768), tok(512), tok(512), tok(128)],
        out_shape=[jax.ShapeDtypeStruct((B, T, 1024), BF16),
                   jax.ShapeDtypeStruct((B, T, 768), BF16),
                   jax.ShapeDtypeStruct((B, T, 512), BF16),
                   jax.ShapeDtypeStruct((B, T, 512), BF16),
                   jax.ShapeDtypeStruct((B, T, 128), F32)],
        compiler_params=_cparams(("parallel", "parallel")),
        name="in_proj",
    )(x, mod, norm_g.reshape(1, D), w_in_p)


def _outproj_body(of_ref, ob_ref, ga_ref, grp_ref, ng_ref, yb_ref, yc_ref, yd_ref, x_ref, mod_ref, w_ref):
    o = of_ref[0] + ob_ref[0]
    hi, lo = _split2(o * o)
    ms = (_dot(hi, grp_ref[...]) + _dot(lo, grp_ref[...])) * (1.0 / DK)
    ya = (o * lax.rsqrt(ms + EPS) * ng_ref[...] * _silu(ga_ref[0].astype(F32))).astype(BF16)
    ycat = jnp.concatenate([ya, yb_ref[0], yc_ref[0], yd_ref[0]], axis=-1)
    out = _dot(ycat, w_ref[...])
    gate = mod_ref[0, :, 2 * D_MODEL:3 * D_MODEL]
    return x_ref[0] + gate * out


def _outproj_kernel(of_ref, ob_ref, ga_ref, grp_ref, ng_ref, yb_ref, yc_ref, yd_ref, x_ref, mod_ref, w_ref, o_ref):
    o_ref[0] = _outproj_body(of_ref, ob_ref, ga_ref, grp_ref, ng_ref, yb_ref, yc_ref, yd_ref, x_ref, mod_ref, w_ref)


def _outproj_final_kernel(of_ref, ob_ref, ga_ref, grp_ref, ng_ref, yb_ref, yc_ref, yd_ref, x_ref, mod_ref, w_ref,
                          fg_ref, o_ref):
    xn = _outproj_body(of_ref, ob_ref, ga_ref, grp_ref, ng_ref, yb_ref, yc_ref, yd_ref, x_ref, mod_ref, w_ref)
    ms = jnp.mean(xn * xn, axis=-1, keepdims=True)
    o_ref[0] = xn * lax.rsqrt(ms + EPS) * fg_ref[...]


def _outproj(o_f, o_b, pa, grp, ng4, yb, yc, yd, x, mod, w_out, final_g, tm):
    B, T, D = x.shape
    nmod = mod.shape[0]
    mod_map = (lambda b, i: (b, 0, 0)) if nmod == B else (lambda b, i: (0, 0, 0))
    tok = lambda n: pl.BlockSpec((1, tm, n), lambda b, i: (b, i, 0))
    in_specs = [tok(256), tok(256),
                pl.BlockSpec((1, tm, 256), lambda b, i: (b, i, 3)),
                pl.BlockSpec((256, 256), lambda b, i: (0, 0)),
                pl.BlockSpec((1, 256), lambda b, i: (0, 0)),
                tok(256), tok(256), tok(256), tok(D),
                pl.BlockSpec((1, 1, 3 * D), mod_map),
                pl.BlockSpec((D, D), lambda b, i: (0, 0))]
    args = [o_f, o_b, pa, grp, ng4, yb, yc, yd, x, mod, w_out]
    kern = _outproj_kernel
    if final_g is not None:
        in_specs.append(pl.BlockSpec((1, D), lambda b, i: (0, 0)))
        args.append(final_g.reshape(1, D))
        kern = _outproj_final_kernel
    return pl.pallas_call(
        kern,
        grid=(B, T // tm),
        in_specs=in_specs,
        out_specs=tok(D),
        out_shape=jax.ShapeDtypeStruct((B, T, D), F32),
        compiler_params=_cparams(("parallel", "parallel")),
        name="out_proj",
    )(*args)


def _gelu_tanh(x):
    return 0.5 * x * (1.0 + jnp.tanh(math.sqrt(2.0 / math.pi) * (x + 0.044715 * (x * x * x))))


def _diag_blocks(r, lane_grp):
    n = r.shape[0] // 4
    out = r[0:n]
    for g in range(1, 4):
        out = jnp.where(lane_grp == g, r[g * n:(g + 1) * n], out)
    return out


def _local_kernel(pb_ref, pc_ref, prev_ref, next_ref, sgw_ref, sgb_ref, sgn_ref, band_ref, pw_ref, ps_ref,
                  yb_ref, yc_ref, *, T, tl):
    i = pl.program_id(1)
    nblk = T // tl
    n = SGU_CHUNK
    lane_grp = lax.broadcasted_iota(jnp.int32, (n, 256), 1) // 64
    row = lax.broadcasted_iota(jnp.int32, (n, 256), 0)
    lo_w = jnp.zeros((n, 256), jnp.int32)
    hi_w = jnp.zeros((n, 256), jnp.int32)
    for g, w in enumerate(POOL_WINDOWS):
        lo_w = jnp.where(lane_grp == g, w // 2, lo_w)
        hi_w = jnp.where(lane_grp == g, w - w // 2, hi_w)
    prev_blk = jnp.where(i > 0, prev_ref[0, :, 0:256], jnp.zeros((HALO, 256), BF16))
    next_blk = jnp.where(i < nblk - 1, next_ref[0, :, 0:256], jnp.zeros((HALO, 256), BF16))
    nch = tl // n
    for j in range(nch):
        r0 = j * n
        uvg = pb_ref[0, r0:r0 + n, :]
        gel = _gelu_tanh(uvg[:, 0:512].astype(F32))
        u = gel[:, 0:256]
        v = gel[:, 256:512]
        vn = v * lax.rsqrt(jnp.mean(v * v, axis=-1, keepdims=True) + EPS) * sgn_ref[...]
        sv = _diag_blocks(_dot(sgw_ref[...], vn.astype(BF16)), lane_grp) + sgb_ref[...]
        yb_ref[0, r0:r0 + n, :] = (u * sv * _silu(uvg[:, 512:768].astype(F32))).astype(BF16)
        xt = pc_ref[0, r0:r0 + n, 0:256]
        pv = prev_blk if j == 0 else pc_ref[0, r0 - HALO:r0, 0:256]
        nx = next_blk if j == nch - 1 else pc_ref[0, r0 + n:r0 + n + HALO, 0:256]
        xcat = jnp.concatenate([pv, xt, nx], axis=0)
        wsum = _diag_blocks(_dot(band_ref[...], xcat), lane_grp)
        t = i * tl + r0 + row
        cnt = (jnp.minimum(t + hi_w, T) - jnp.maximum(t - lo_w, 0)).astype(F32)
        pooled = wsum / cnt - xt.astype(F32)
        yc = _dot(pooled.astype(BF16), pw_ref[...]) * ps_ref[...]
        yc_ref[0, r0:r0 + n, :] = (yc * _silu(pc_ref[0, r0:r0 + n, 256:512].astype(F32))).astype(BF16)


def _local(pb, pc, sgw, sgb, sgn, band, pw_bd, pscale, tl):
    B, T, _ = pb.shape
    hb = tl // HALO
    nh = T // HALO
    const = lambda shape: pl.BlockSpec(shape, lambda b, i: (0,) * len(shape))
    return pl.pallas_call(
        functools.partial(_local_kernel, T=T, tl=tl),
        grid=(B, T // tl),
        in_specs=[pl.BlockSpec((1, tl, 768), lambda b, i: (b, i, 0)),
                  pl.BlockSpec((1, tl, 512), lambda b, i: (b, i, 0)),
                  pl.BlockSpec((1, HALO, 512), lambda b, i: (b, jnp.maximum(i * hb - 1, 0), 0)),
                  pl.BlockSpec((1, HALO, 512), lambda b, i: (b, jnp.minimum((i + 1) * hb, nh - 1), 0)),
                  const((4 * SGU_CHUNK, SGU_CHUNK)), const((SGU_CHUNK, 256)), const((1, 256)),
                  const((4 * SGU_CHUNK, SGU_CHUNK + 2 * HALO)), const((256, 256)), const((1, 256))],
        out_specs=[pl.BlockSpec((1, tl, 256), lambda b, i: (b, i, 0)),
                   pl.BlockSpec((1, tl, 256), lambda b, i: (b, i, 0))],
        out_shape=[jax.ShapeDtypeStruct((B, T, 256), BF16), jax.ShapeDtypeStruct((B, T, 256), BF16)],
        compiler_params=_cparams(("parallel", "parallel")),
        name="sgu_pool",
    )(pb, pc, pc, pc, sgw, sgb, sgn, band, pw_bd, pscale)


def _dft1_kernel(pd_ref, cs_ref, z_ref):
    z = _dot(pd_ref[0, :, 0:256], cs_ref[...])
    z_ref[0, 0] = z[:, 0:256].astype(BF16)
    z_ref[0, 1] = z[:, 256:512].astype(BF16)


def _dft2_kernel(f_ref, z_ref, pd_ref, fw_ref, y_ref, *, inv_norm):
    f = _dot(f_ref[...], z_ref[0]) * inv_norm
    y = _dot(f.astype(BF16), fw_ref[...])
    y_ref[0] = (y * _silu(pd_ref[0, :, 256:512].astype(F32))).astype(BF16)


def _fourier(pd, cs, fmat, fw_bd, tm, tq):
    B, T, _ = pd.shape
    z = pl.pallas_call(
        _dft1_kernel,
        grid=(B, T // tm),
        in_specs=[pl.BlockSpec((1, tm, 512), lambda b, i: (b, i, 0)),
                  pl.BlockSpec((256, 512), lambda b, i: (0, 0))],
        out_specs=pl.BlockSpec((1, 2, tm, 256), lambda b, i: (b, 0, i, 0)),
        out_shape=jax.ShapeDtypeStruct((B, 2, T, 256), BF16),
        compiler_params=_cparams(("parallel", "parallel")),
        name="dft_channels",
    )(pd, cs)
    z = z.reshape(B, 2 * T, 256)
    return pl.pallas_call(
        functools.partial(_dft2_kernel, inv_norm=1.0 / math.sqrt(T * C_D)),
        grid=(T // tq, B),
        in_specs=[pl.BlockSpec((tq, 2 * T), lambda i, b: (i, 0)),
                  pl.BlockSpec((1, 2 * T, 256), lambda i, b: (b, 0, 0)),
                  pl.BlockSpec((1, tq, 512), lambda i, b: (b, i, 0)),
                  pl.BlockSpec((256, 256), lambda i, b: (0, 0))],
        out_specs=pl.BlockSpec((1, tq, 256), lambda i, b: (b, i, 0)),
        out_shape=jax.ShapeDtypeStruct((B, T, 256), BF16),
        compiler_params=_cparams(("parallel", "parallel")),
        name="dft_positions",
    )(fmat, z, pd, fw_bd)


TRI_UNITS = 128
TRI_STRIDE = DN_CHUNK + 8


def _bd_mask():
    r = lax.broadcasted_iota(jnp.int32, (256, 256), 0)
    c = lax.broadcasted_iota(jnp.int32, (256, 256), 1)
    return (r // 64) == (c // 64)


def _expand(x, bd):
    return jnp.where(bd, jnp.concatenate([x, x, x, x], axis=0), jnp.zeros((), x.dtype))


def _delta_prep_kernel(qkv_ref, prev_ref, next_ref, ba_ref, cw_ref, par_ref, sel_ref, grp_ref,
                       d0_ref, d1_ref, a_ref, gl_ref, *, T, tc):
    j = pl.program_id(1)
    nblk = T // tc
    x = qkv_ref[0].astype(F32)
    prev = jnp.where(j > 0, prev_ref[0].astype(F32), 0.0)
    nxt = jnp.where(j < nblk - 1, next_ref[0].astype(F32), 0.0)
    xcat = jnp.concatenate([prev, x, nxt], axis=0)
    n = tc + 2 * HALO
    acc = None
    for tap in range(CONV_W):
        s = (CONV_W // 2 - tap) % n
        xs = xcat if s == 0 else pltpu.roll(xcat, s, axis=0)
        term = xs[HALO:HALO + tc] * cw_ref[tap:tap + 1, :]
        acc = term if acc is None else acc + term
    c = _silu(acc)
    cq, ck, v = c[:, 0:256], c[:, 256:512], c[:, 512:768]
    grp = grp_ref[...]
    qh, ql = _split2(cq * cq)
    kh, kl = _split2(ck * ck)
    q = cq * lax.rsqrt(_dot(qh, grp) + _dot(ql, grp) + EPS) * (DK ** -0.5)
    k = ck * lax.rsqrt(_dot(kh, grp) + _dot(kl, grp) + EPS)
    ba = ba_ref[0]
    z = ba + par_ref[1:2, :]
    softplus = jnp.maximum(z, 0.0) + jnp.log1p(jnp.exp(-jnp.abs(z)))
    lane = lax.broadcasted_iota(jnp.int32, ba.shape, 1)
    bg = jnp.where(lane < 2 * H_A, jax.nn.sigmoid(ba), -jnp.exp(par_ref[0:1, :]) * softplus)
    bgx = _dot_exact_rhs(bg, sel_ref[...])
    rr = lax.broadcasted_iota(jnp.int32, (tc, tc), 0)
    cc = lax.broadcasted_iota(jnp.int32, (tc, tc), 1)
    same = (rr // DN_CHUNK) == (cc // DN_CHUNK)
    tri_f = jnp.where(same & (rr >= cc), 1.0, 0.0).astype(BF16)
    tri_b = jnp.where(same & (rr <= cc), 1.0, 0.0).astype(BF16)
    gcum = (_dot_exact_lhs(tri_f, bgx[:, 512:768]), _dot_exact_lhs(tri_b, bgx[:, 768:1024]))

    bd = _bd_mask()
    r64 = lax.broadcasted_iota(jnp.int32, (DN_CHUNK, 256), 0)
    c64 = lax.broadcasted_iota(jnp.int32, (DN_CHUNK, 256), 1) % 64
    diag64 = r64 == c64
    pad_rows = jnp.zeros((TRI_STRIDE - DN_CHUNK, 128), F32)
    out_refs = (d0_ref, d1_ref)
    for ch in range(tc // DN_CHUNK):
        r0 = ch * DN_CHUNK
        a0 = ch * TRI_STRIDE
        kx = k[r0:r0 + DN_CHUNK]
        qx = q[r0:r0 + DN_CHUNK]
        vx = v[r0:r0 + DN_CHUNK]
        kb = kx.astype(BF16)
        kq = _dot_nt(jnp.concatenate([kb, qx.astype(BF16)], axis=0), _expand(kb, bd))
        kk, qk = kq[0:DN_CHUNK], kq[DN_CHUNK:2 * DN_CHUNK]
        for d in range(2):
            gc = gcum[d][r0:r0 + DN_CHUNK]
            beta = bgx[r0:r0 + DN_CHUNK, d * 256:(d + 1) * 256]
            rowv = jnp.sum(jnp.where(diag64, gc, 0.0), axis=0, keepdims=True)
            incl = (r64 >= c64) if d == 0 else (r64 <= c64)
            strict = (r64 > c64) if d == 0 else (r64 < c64)
            decay = jnp.where(incl, jnp.exp(jnp.where(incl, gc - rowv, 0.0)), 0.0)
            a_c = jnp.where(strict, beta * kk * decay, 0.0)
            for half in range(2):
                a_ref[d, half, a0:a0 + DN_CHUNK, :] = a_c[:, 128 * half:128 * half + 128]
                a_ref[d, half, a0 + DN_CHUNK:a0 + TRI_STRIDE, :] = pad_rows
            egc = jnp.exp(gc)
            g_last = gc[DN_CHUNK - 1:DN_CHUNK] if d == 0 else gc[0:1]
            o = out_refs[d]
            o[0, r0:r0 + DN_CHUNK, 0:256] = (vx * beta).astype(BF16)
            o[0, r0:r0 + DN_CHUNK, 256:512] = (kx * beta * egc).astype(BF16)
            o[0, r0:r0 + DN_CHUNK, 512:768] = (qx * egc).astype(BF16)
            o[0, r0:r0 + DN_CHUNK, 768:1024] = (kx * jnp.exp(g_last - gc)).astype(BF16)
            o[0, r0:r0 + DN_CHUNK, 1024:1280] = (qk * decay).astype(BF16)
            gl_ref[0, d, ch] = jnp.exp(g_last)


def _delta_prep(pa, ba, conv_w, par, sel, grp, tc):
    B, T, _ = pa.shape
    hb = tc // HALO
    nh = T // HALO
    nt = T // tc
    nch = tc // DN_CHUNK
    const = lambda shape: pl.BlockSpec(shape, lambda b, i: (0,) * len(shape))
    return pl.pallas_call(
        functools.partial(_delta_prep_kernel, T=T, tc=tc),
        grid=(B, nt),
        in_specs=[pl.BlockSpec((1, tc, 768), lambda b, i: (b, i, 0)),
                  pl.BlockSpec((1, HALO, 768), lambda b, i: (b, jnp.maximum(i * hb - 1, 0), 0)),
                  pl.BlockSpec((1, HALO, 768), lambda b, i: (b, jnp.minimum((i + 1) * hb, nh - 1), 0)),
                  pl.BlockSpec((1, tc, 128), lambda b, i: (b, i, 0)),
                  const((CONV_W, 768)), const((2, 128)), const((128, 1024)), const((256, 256))],
        out_specs=[pl.BlockSpec((1, tc, DREC), lambda b, i: (b, i, 0)),
                   pl.BlockSpec((1, tc, DREC), lambda b, i: (b, i, 0)),
                   pl.BlockSpec((2, 2, nch * TRI_STRIDE, 128), lambda b, i: (0, 0, b * nt + i, 0)),
                   pl.BlockSpec((1, 2, nch, 1, 256), lambda b, i: (b, 0, i, 0, 0))],
        out_shape=[jax.ShapeDtypeStruct((B, T, DREC), BF16),
                   jax.ShapeDtypeStruct((B, T, DREC), BF16),
                   jax.ShapeDtypeStruct((2, 2, B * nt * nch * TRI_STRIDE, 128), F32),
                   jax.ShapeDtypeStruct((B, 2, T // DN_CHUNK, 1, 256), F32)],
        compiler_params=_cparams(("parallel", "parallel")),
        name="delta_prep",
    )(pa, pa, pa, ba, conv_w, par, sel, grp)


TRI_ROWS = TRI_UNITS * TRI_STRIDE


def _tri_rows(x_ref, lower):
    sub = lax.broadcasted_iota(jnp.int32, (8, TRI_UNITS), 0)
    nb = DN_CHUNK // 8
    for ib in (range(nb) if lower else reversed(range(nb))):
        segs = range(ib + 1) if lower else range(ib, nb)
        live = range(ib + 1) if lower else range(ib, nb)

        def row(step, carry, ib=ib, segs=segs, live=live):
            i = 8 * ib + (step if lower else 7 - step)
            for heads in ((0, 1), (2, 3)):
                acc = {(h, cv): jnp.zeros((8, TRI_UNITS), F32) for h in heads for cv in live}
                for jb in segs:
                    cvs = range(jb + 1) if lower else range(jb, nb)
                    for j in range(8 * jb, 8 * jb + 8):
                        for h in heads:
                            coef = jnp.broadcast_to(x_ref[i, pl.ds(h * DK + j, 1), :], (8, TRI_UNITS))
                            for cv in cvs:
                                acc[h, cv] = acc[h, cv] + coef * x_ref[j, h * DK + 8 * cv:h * DK + 8 * cv + 8, :]
                for h in heads:
                    for cv in range(nb):
                        val = (jnp.where(sub + 8 * cv == i, 1.0, -acc[h, cv]) if cv in live
                               else jnp.zeros((8, TRI_UNITS), F32))
                        x_ref[i, h * DK + 8 * cv:h * DK + 8 * cv + 8, :] = val
            return carry

        lax.fori_loop(0, 8, row, 0)


def _tri_inverse_kernel(a_ref, t_ref, x_ref):
    d = pl.program_id(0)

    def load(i, carry):
        for half in range(2):
            x_ref[i, 128 * half:128 * half + 128, :] = a_ref[0, half, pl.ds(i, TRI_UNITS, stride=TRI_STRIDE), :].T
        return carry

    lax.fori_loop(0, DN_CHUNK, load, 0)

    @pl.when(d == 0)
    def _():
        _tri_rows(x_ref, True)

    @pl.when(d == 1)
    def _():
        _tri_rows(x_ref, False)

    def store(i, carry):
        for half in range(2):
            t_ref[0, half, pl.ds(i, TRI_UNITS, stride=TRI_STRIDE), :] = x_ref[i, 128 * half:128 * half + 128, :].T
        return carry

    lax.fori_loop(0, DN_CHUNK, store, 0)
    for half in range(2):
        for p in range(DN_CHUNK, TRI_STRIDE):
            t_ref[0, half, pl.ds(p, TRI_UNITS, stride=TRI_STRIDE), :] = jnp.zeros((TRI_UNITS, 128), F32)


def _tri_inverse(a):
    rows = a.shape[2]
    spec = pl.BlockSpec((1, 2, TRI_ROWS, 128), lambda d, i: (d, 0, i, 0))
    return pl.pallas_call(
        _tri_inverse_kernel,
        grid=(2, rows // TRI_ROWS),
        in_specs=[spec],
        out_specs=spec,
        out_shape=jax.ShapeDtypeStruct(a.shape, F32),
        scratch_shapes=[pltpu.VMEM((DN_CHUNK, 256, TRI_UNITS), F32)],
        compiler_params=_cparams(("parallel", "parallel")),
        name="tri_inverse",
    )(a)


def _delta_scan_kernel_with_state(d0_ref, d1_ref, tf_ref, tb_ref, glf_ref, glb_ref, s0_ref, of_ref, ob_ref,
                                  sfin_ref, s_ref, *, tc, nb):
    _delta_scan_kernel(s0_ref, d0_ref, d1_ref, tf_ref, tb_ref, glf_ref, glb_ref, of_ref, ob_ref, sfin_ref, s_ref,
                       tc=tc, nb=nb)


def _delta_scan_kernel(s0_ref, d0_ref, d1_ref, tf_ref, tb_ref, glf_ref, glb_ref, of_ref, ob_ref, sfin_ref,
                       s_ref, *, tc, nb):
    j = pl.program_id(1)
    nblk = pl.num_programs(1)
    nch = tc // DN_CHUNK

    @pl.when(j == 0)
    def _():
        if s0_ref is None:
            s_ref[...] = jnp.zeros(s_ref.shape, F32)
        else:
            zero = jnp.zeros((DK, DK), F32)
            for q in range(nb):
                for d in range(2):
                    s_ref[q, d] = jnp.concatenate(
                        [jnp.concatenate([s0_ref[q, d, h] if g == h else zero for g in range(H_A)], axis=1)
                         for h in range(H_A)], axis=0)

    bd = _bd_mask()
    rec = (d0_ref, d1_ref)
    tri = (tf_ref, tb_ref)
    glr = (glf_ref, glb_ref)
    out = (of_ref, ob_ref)
    chains = [(q, d) for q in range(nb) for d in range(2)]

    def chunk_of(d, step):
        return step if d == 0 else nch - 1 - step

    def rows_of(c):
        return slice(c * DN_CHUNK, (c + 1) * DN_CHUNK)

    uw = {}
    for step in range(nch):
        for q, d in chains:
            c = chunk_of(d, step)
            trows = slice(c * TRI_STRIDE, c * TRI_STRIDE + DN_CHUNK)
            tinv = jnp.concatenate([tri[d][0, 0, q, trows, :], tri[d][0, 1, q, trows, :]], axis=1).astype(BF16)
            rhs = jnp.concatenate([_expand(rec[d][q, rows_of(c), 0:256], bd),
                                   _expand(rec[d][q, rows_of(c), 256:512], bd)], axis=1)
            uw[q, d, step] = _dot(tinv, rhs)

    for step in range(nch):
        s = {k: s_ref[k[0], k[1]] for k in chains}
        ws_qs = {}
        for q, d in chains:
            r = rows_of(chunk_of(d, step))
            wq = jnp.concatenate([uw[q, d, step][:, 256:512].astype(BF16), rec[d][q, r, 512:768]], axis=0)
            ws_qs[q, d] = _dot(wq, s[q, d].astype(BF16))
        v_new = {k: (uw[k[0], k[1], step][:, 0:256] - ws_qs[k][0:DN_CHUNK]).astype(BF16) for k in chains}
        for q, d in chains:
            c = chunk_of(d, step)
            r = rows_of(c)
            out[d][q, r, :] = ws_qs[q, d][DN_CHUNK:2 * DN_CHUNK] + _dot(rec[d][q, r, 1024:1280], _expand(v_new[q, d], bd))
            upd = _dot_tn(rec[d][q, r, 768:1024], v_new[q, d])
            s_ref[q, d] = s[q, d] * glr[d][q, 0, c] + jnp.where(bd, upd, 0.0)

    @pl.when(j == nblk - 1)
    def _():
        for q in range(nb):
            for d in range(2):
                s = s_ref[q, d]
                for h in range(H_A):
                    sfin_ref[q, d, h] = s[h * DK:(h + 1) * DK, h * DK:(h + 1) * DK]


def _delta_scan(d0, d1, tinv, gl, s0, tc, nb):
    B, T, _ = d0.shape
    nblk = T // tc
    nch = tc // DN_CHUNK
    tinv = tinv.reshape(2, 2, B, nblk * nch * TRI_STRIDE, 128)
    state_spec = pl.BlockSpec((nb, 2, H_A, DK, DK), lambda b, j: (b, 0, 0, 0, 0))
    in_specs = [pl.BlockSpec((nb, tc, DREC), lambda b, j: (b, j, 0)),
                pl.BlockSpec((nb, tc, DREC), lambda b, j: (b, nblk - 1 - j, 0)),
                pl.BlockSpec((1, 2, nb, nch * TRI_STRIDE, 128), lambda b, j: (0, 0, b, j, 0)),
                pl.BlockSpec((1, 2, nb, nch * TRI_STRIDE, 128), lambda b, j: (1, 0, b, nblk - 1 - j, 0)),
                pl.BlockSpec((nb, 1, nch, 1, 256), lambda b, j: (b, 0, j, 0, 0)),
                pl.BlockSpec((nb, 1, nch, 1, 256), lambda b, j: (b, 1, nblk - 1 - j, 0, 0))]
    args = [d0, d1, tinv, tinv, gl, gl]
    if s0 is None:
        kern = functools.partial(_delta_scan_kernel, None, tc=tc, nb=nb)
    else:
        kern = functools.partial(_delta_scan_kernel_with_state, tc=tc, nb=nb)
        in_specs.append(state_spec)
        args.append(s0)
    return pl.pallas_call(
        kern,
        grid=(B // nb, nblk),
        in_specs=in_specs,
        out_specs=[pl.BlockSpec((nb, tc, 256), lambda b, j: (b, j, 0)),
                   pl.BlockSpec((nb, tc, 256), lambda b, j: (b, nblk - 1 - j, 0)),
                   state_spec],
        out_shape=[jax.ShapeDtypeStruct((B, T, 256), F32),
                   jax.ShapeDtypeStruct((B, T, 256), F32),
                   jax.ShapeDtypeStruct((B, 2, H_A, DK, DK), F32)],
        scratch_shapes=[pltpu.VMEM((nb, 2, 256, 256), F32)],
        compiler_params=_cparams(("parallel", "arbitrary")),
        name="delta_scan",
    )(*args)


def _block_diag4(w):
    n = w.shape[-1]
    eye = jnp.eye(4, dtype=w.dtype)
    return jnp.einsum("gij,gh->gihj", w, eye).reshape(4 * n, 4 * n)


def _dft_matrix(T):
    w = 2.0 * math.pi / T
    t = jnp.arange(T, dtype=jnp.int32)[None, None, :]
    a = jnp.arange(T // 128, dtype=jnp.int32)[:, None, None]
    b = jnp.arange(128, dtype=jnp.int32)[None, :, None]
    ang_a = ((128 * a * t) % T).astype(F32) * w
    ang_b = ((b * t) % T).astype(F32) * w
    ca, sa, cb, sb = jnp.cos(ang_a), jnp.sin(ang_a), jnp.cos(ang_b), jnp.sin(ang_b)
    cos_m = (ca * cb - sa * sb).reshape(T, T)
    sin_m = (sa * cb + ca * sb).reshape(T, T)
    return jnp.concatenate([cos_m, sin_m], axis=1).astype(BF16)


def _channel_dft():
    c = jnp.arange(C_D, dtype=jnp.int32)
    ang = ((c[:, None] * c[None, :]) % C_D).astype(F32) * (2.0 * math.pi / C_D)
    cos4 = _block_diag4(jnp.broadcast_to(jnp.cos(ang), (G_D, C_D, C_D)))
    sin4 = _block_diag4(jnp.broadcast_to(jnp.sin(ang), (G_D, C_D, C_D)))
    return jnp.concatenate([cos4, -sin4], axis=1).astype(BF16)


def _pool_band():
    n = SGU_CHUNK
    t = jnp.arange(n)[:, None]
    s = jnp.arange(n + 2 * HALO)[None, :] - HALO
    blocks = [((s >= t - w // 2) & (s < t + w - w // 2)) for w in POOL_WINDOWS]
    return jnp.concatenate(blocks, axis=0).astype(BF16)


def _gate_selector():
    src = jnp.arange(128)[:, None]
    dst = jnp.arange(1024)[None, :]
    kind, d, h = dst // 512, (dst // 256) % 2, (dst // 64) % 4
    return (src == kind * 8 + d * 4 + h).astype(BF16)


def _grid_pos_embed(T):
    rows = T // GRID_W
    r = jnp.repeat(jnp.arange(rows, dtype=F32), GRID_W)
    col = jnp.tile(jnp.arange(GRID_W, dtype=F32), rows)
    n_freq = D_MODEL // 4
    freqs = jnp.power(POS_BASE, -jnp.arange(n_freq, dtype=F32) / n_freq)
    ar = r[:, None] * freqs[None]
    ac = col[:, None] * freqs[None]
    return jnp.concatenate([jnp.sin(ar), jnp.cos(ar), jnp.sin(ac), jnp.cos(ac)], axis=-1)


def _layer_consts(l, w):
    win = w["w_in"][l]
    w_in_p = jnp.concatenate([win[:, 0:1024], win[:, 1040:2832], win[:, 1024:1040],
                              jnp.zeros((D_MODEL, P_PAD - 2832), F32)], axis=1).astype(BF16)
    par = jnp.zeros((2, 128), F32)
    par = par.at[0, 8:16].set(w["a_log"][l].reshape(8)).at[1, 8:16].set(w["dt_bias"][l].reshape(8))
    return dict(
        norm_g=w["norm_g"][l], w_in_p=w_in_p, conv_w=w["conv_qkv"][l], par=par,
        dn_g4=jnp.tile(w["dn_norm_g"][l], H_A).reshape(1, 256),
        sgw=w["sgu_w"][l].reshape(G_B * SGU_CHUNK, SGU_CHUNK).astype(BF16),
        sgb=jnp.repeat(w["sgu_b"][l].T, W_B // G_B, axis=1),
        sgn=w["sgu_norm_g"][l].reshape(1, W_B),
        pw_bd=_block_diag4(w["pool_w"][l]).astype(BF16),
        pscale=w["pool_scale"][l].reshape(1, W_C),
        fw_bd=_block_diag4(w["fourier_w"][l]).astype(BF16),
        w_out=w["w_out"][l].astype(BF16),
    )


def _trunk_layer(x, mod, s0, lc, shared, fmat, final_g):
    B, T, _ = x.shape
    tm = min(T, 512)
    pa, pb, pc, pd, ba = _inproj(x, mod, lc["norm_g"], lc["w_in_p"], tm)
    d0, d1, a_tri, gl = _delta_prep(pa, ba, lc["conv_w"], lc["par"], shared["sel"], shared["grp"], 256)
    o_f, o_b, s_fin = _delta_scan(d0, d1, _tri_inverse(a_tri), gl, s0, 256, 2)
    yb, yc = _local(pb, pc, lc["sgw"], lc["sgb"], lc["sgn"], shared["band"], lc["pw_bd"], lc["pscale"], tm)
    yd = _fourier(pd, shared["cs"], fmat, lc["fw_bd"], tm, tm)
    xn = _outproj(o_f, o_b, pa, shared["grp"], lc["dn_g4"], yb, yc, yd, x, mod, lc["w_out"], final_g, tm)
    return xn, s_fin


def kernel(x_prompt, x_sample, state_delta, c, c_ctx, ada_w, ada_b, norm_g, w_in, conv_qkv, a_log, dt_bias,
           dn_norm_g, sgu_norm_g, sgu_w, sgu_b, pool_w, pool_scale, fourier_w, w_out, final_norm_g):
    w = dict(norm_g=norm_g, w_in=w_in, conv_qkv=conv_qkv, a_log=a_log, dt_bias=dt_bias, dn_norm_g=dn_norm_g,
             sgu_norm_g=sgu_norm_g, sgu_w=sgu_w, sgu_b=sgu_b, pool_w=pool_w, pool_scale=pool_scale,
             fourier_w=fourier_w, w_out=w_out)
    bp, tp, _ = x_prompt.shape
    bs, ts, _ = x_sample.shape
    conds = jnp.concatenate([c, c_ctx[None], jnp.zeros((16 - bs - 1, D_MODEL), F32)], axis=0)
    mod = _modulation(conds, ada_w, ada_b)
    ones64 = jnp.ones((H_A, DK, DK), F32)
    shared = dict(sel=_gate_selector(), grp=_block_diag4(ones64).astype(BF16), band=_pool_band(),
                  cs=_channel_dft())
    fmat_p = _dft_matrix(tp)
    fmat_s = _dft_matrix(ts)
    layers = [_layer_consts(l, w) for l in range(DEPTH)]

    xp = x_prompt
    xs = _add_pos(x_sample, _grid_pos_embed(ts), min(ts, 512))
    ctx_states = []
    for l in range(DEPTH):
        fg = final_norm_g if l == DEPTH - 1 else None
        xp, s_l = _trunk_layer(xp, mod[l, bs:bs + 1].reshape(1, 1, 3 * D_MODEL), None, layers[l], shared,
                               fmat_p, fg)
        ctx_states.append(s_l)
        xs, _ = _trunk_layer(xs, mod[l, 0:bs].reshape(bs, 1, 3 * D_MODEL), state_delta[:, l],
                             layers[l], shared, fmat_s, fg)
    return xp, xs, jnp.stack(ctx_states, axis=1)
```

```python
import functools
import math

import numpy as np
import jax
import jax.numpy as jnp
from jax import lax
from jax.experimental import pallas as pl
from jax.experimental.pallas import tpu as pltpu

F32 = jnp.float32
BF16 = jnp.bfloat16

D_MODEL = 1024
DEPTH = 4
GRID_W = 64
POS_BASE = 10000.0
H_A = 4
DK = 64
W_A = 256
CONV_W = 4
DN_CHUNK = 64
W_B = 256
G_B = 4
SGU_CHUNK = 128
W_C = 256
POOL_WINDOWS = (2, 4, 8, 16)
W_D = 256
G_D = 4
C_D = 64
EPS = 1e-6

P_MAIN = 3072
P_PAD = P_MAIN + 128
HALO = 16
DREC = 5 * W_A
SCAN_SEQS = 4

VMEM_LIMIT = 56 * 1024 * 1024


def _cparams(sem):
    return pltpu.CompilerParams(dimension_semantics=sem, vmem_limit_bytes=VMEM_LIMIT)


def _silu(x):
    return x * jax.nn.sigmoid(x)


def _split2(x):
    hi = x.astype(BF16)
    lo = (x - hi.astype(F32)).astype(BF16)
    return hi, lo


def _split3(x):
    hi = x.astype(BF16)
    r = x - hi.astype(F32)
    mid = r.astype(BF16)
    lo = (r - mid.astype(F32)).astype(BF16)
    return hi, mid, lo


def _dot(a, b):
    return jnp.dot(a, b, preferred_element_type=F32)


def _dot_nt(a, b):
    return lax.dot_general(a, b, (((1,), (1,)), ((), ())), preferred_element_type=F32)


def _dot_tn(a, b):
    return lax.dot_general(a, b, (((0,), (0,)), ((), ())), preferred_element_type=F32)


def _dot_exact_rhs(x, sel):
    hi, mid, lo = _split3(x)
    return _dot(hi, sel) + _dot(mid, sel) + _dot(lo, sel)


def _dot_exact_lhs(sel, x):
    hi, mid, lo = _split3(x)
    return _dot(sel, hi) + _dot(sel, mid) + _dot(sel, lo)


def _mod_kernel(c_ref, w_ref, b_ref, o_ref):
    a = _silu(c_ref[...]).astype(BF16)
    o_ref[0] = _dot(a, w_ref[0].astype(BF16)) + b_ref[0]


def _modulation(conds, ada_w, ada_b):
    nc = conds.shape[0]
    tn = 1024
    return pl.pallas_call(
        _mod_kernel,
        grid=(DEPTH, 3 * D_MODEL // tn),
        in_specs=[pl.BlockSpec((nc, D_MODEL), lambda l, n: (0, 0)),
                  pl.BlockSpec((1, D_MODEL, tn), lambda l, n: (l, 0, n)),
                  pl.BlockSpec((1, 1, tn), lambda l, n: (l, 0, n))],
        out_specs=pl.BlockSpec((1, nc, tn), lambda l, n: (l, 0, n)),
        out_shape=jax.ShapeDtypeStruct((DEPTH, nc, 3 * D_MODEL), F32),
        compiler_params=_cparams(("parallel", "parallel")),
        name="adaln_mod",
    )(conds, ada_w, ada_b.reshape(DEPTH, 1, 3 * D_MODEL))


def _inproj_body(x, mod_ref, ng_ref, w_ref, pa_ref, pb_ref, pc_ref, gd_ref, z_ref, ba_ref):
    ms = jnp.mean(x * x, axis=-1, keepdims=True)
    y = x * lax.rsqrt(ms + EPS) * ng_ref[...]
    shift = mod_ref[0, :, 0:D_MODEL]
    scale = mod_ref[0, :, D_MODEL:2 * D_MODEL]
    h = (y * (1.0 + scale) + shift).astype(BF16)
    pa_ref[0] = _dot(h, w_ref[:, 0:1024]).astype(BF16)
    pb_ref[0] = _dot(h, w_ref[:, 1024:1792]).astype(BF16)
    pc_ref[0] = _dot(h, w_ref[:, 1792:2304]).astype(BF16)
    gd_ref[0] = _dot(h, w_ref[:, 2304:2560]).astype(BF16)
    z = _dot(h, w_ref[:, 2560:3072])
    z_ref[0, 0] = z[:, 0:256].astype(BF16)
    z_ref[0, 1] = z[:, 256:512].astype(BF16)
    ba_ref[0] = _dot(h, w_ref[:, 3072:3200])


def _inproj_kernel(x_ref, mod_ref, ng_ref, w_ref, pa_ref, pb_ref, pc_ref, gd_ref, z_ref, ba_ref):
    _inproj_body(x_ref[0], mod_ref, ng_ref, w_ref, pa_ref, pb_ref, pc_ref, gd_ref, z_ref, ba_ref)


def _inproj_pos_kernel(x_ref, pe_ref, mod_ref, ng_ref, w_ref, x0_ref, pa_ref, pb_ref, pc_ref, gd_ref, z_ref, ba_ref):
    x = x_ref[0] + pe_ref[...]
    x0_ref[0] = x
    _inproj_body(x, mod_ref, ng_ref, w_ref, pa_ref, pb_ref, pc_ref, gd_ref, z_ref, ba_ref)


def _inproj(x, pe, mod, norm_g, w_in_p, tm):
    B, T, D = x.shape
    nmod = mod.shape[0]
    mod_map = (lambda b, i: (b, 0, 0)) if nmod == B else (lambda b, i: (0, 0, 0))
    tok = lambda n: pl.BlockSpec((1, tm, n), lambda b, i: (b, i, 0))
    in_specs = [pl.BlockSpec((1, 1, 3 * D), mod_map),
                pl.BlockSpec((1, D), lambda b, i: (0, 0)),
                pl.BlockSpec((D, P_PAD), lambda b, i: (0, 0))]
    out_specs = [tok(1024), tok(768), tok(512), tok(256),
                 pl.BlockSpec((1, 2, tm, 256), lambda b, i: (b, 0, i, 0)), tok(128)]
    out_shape = [jax.ShapeDtypeStruct((B, T, 1024), BF16),
                 jax.ShapeDtypeStruct((B, T, 768), BF16),
                 jax.ShapeDtypeStruct((B, T, 512), BF16),
                 jax.ShapeDtypeStruct((B, T, 256), BF16),
                 jax.ShapeDtypeStruct((B, 2, T, 256), BF16),
                 jax.ShapeDtypeStruct((B, T, 128), F32)]
    args = [mod, norm_g.reshape(1, D), w_in_p]
    if pe is None:
        kern, in_specs, args = _inproj_kernel, [tok(D)] + in_specs, [x] + args
    else:
        kern = _inproj_pos_kernel
        in_specs = [tok(D), pl.BlockSpec((tm, D), lambda b, i: (i, 0))] + in_specs
        args = [x, pe] + args
        out_specs = [tok(D)] + out_specs
        out_shape = [jax.ShapeDtypeStruct((B, T, D), F32)] + out_shape
    return pl.pallas_call(
        kern,
        grid=(B, T // tm),
        in_specs=in_specs,
        out_specs=out_specs,
        out_shape=out_shape,
        compiler_params=_cparams(("parallel", "parallel")),
        name="in_proj",
    )(*args)


def _fold_kernel(w_ref, cs_ref, o_ref):
    o_ref[0] = _dot_exact_rhs(w_ref[0], cs_ref[...])


def _fold_channel_dft(w_xd, cs):
    return pl.pallas_call(
        _fold_kernel,
        grid=(DEPTH,),
        in_specs=[pl.BlockSpec((1, D_MODEL, 256), lambda l: (l, 0, 0)),
                  pl.BlockSpec((256, 512), lambda l: (0, 0))],
        out_specs=pl.BlockSpec((1, D_MODEL, 512), lambda l: (l, 0, 0)),
        out_shape=jax.ShapeDtypeStruct((DEPTH, D_MODEL, 512), F32),
        compiler_params=_cparams(("parallel",)),
        name="fold_channel_dft",
    )(w_xd, cs)


def _outproj_body(of_ref, ob_ref, ga_ref, grp_ref, ng_ref, yb_ref, yc_ref, yd_ref, x_ref, mod_ref, w_ref):
    o = of_ref[0] + ob_ref[0]
    ms = _dot((o * o).astype(BF16), grp_ref[...]) * (1.0 / DK)
    ya = (o * lax.rsqrt(ms + EPS) * ng_ref[...] * _silu(ga_ref[0].astype(F32))).astype(BF16)
    ycat = jnp.concatenate([ya, yb_ref[0], yc_ref[0], yd_ref[0]], axis=-1)
    out = _dot(ycat, w_ref[...])
    gate = mod_ref[0, :, 2 * D_MODEL:3 * D_MODEL]
    return x_ref[0] + gate * out


def _outproj_kernel(of_ref, ob_ref, ga_ref, grp_ref, ng_ref, yb_ref, yc_ref, yd_ref, x_ref, mod_ref, w_ref, o_ref):
    o_ref[0] = _outproj_body(of_ref, ob_ref, ga_ref, grp_ref, ng_ref, yb_ref, yc_ref, yd_ref, x_ref, mod_ref, w_ref)


def _outproj_final_kernel(of_ref, ob_ref, ga_ref, grp_ref, ng_ref, yb_ref, yc_ref, yd_ref, x_ref, mod_ref, w_ref,
                          fg_ref, o_ref):
    xn = _outproj_body(of_ref, ob_ref, ga_ref, grp_ref, ng_ref, yb_ref, yc_ref, yd_ref, x_ref, mod_ref, w_ref)
    ms = jnp.mean(xn * xn, axis=-1, keepdims=True)
    o_ref[0] = xn * lax.rsqrt(ms + EPS) * fg_ref[...]


def _outproj(o_f, o_b, pa, grp, ng4, yb, yc, yd, x, mod, w_out, final_g, tm):
    B, T, D = x.shape
    nmod = mod.shape[0]
    mod_map = (lambda b, i: (b, 0, 0)) if nmod == B else (lambda b, i: (0, 0, 0))
    tok = lambda n: pl.BlockSpec((1, tm, n), lambda b, i: (b, i, 0))
    in_specs = [tok(256), tok(256),
                pl.BlockSpec((1, tm, 256), lambda b, i: (b, i, 3)),
                pl.BlockSpec((256, 256), lambda b, i: (0, 0)),
                pl.BlockSpec((1, 256), lambda b, i: (0, 0)),
                tok(256), tok(256), tok(256), tok(D),
                pl.BlockSpec((1, 1, 3 * D), mod_map),
                pl.BlockSpec((D, D), lambda b, i: (0, 0))]
    args = [o_f, o_b, pa, grp, ng4, yb, yc, yd, x, mod, w_out]
    kern = _outproj_kernel
    if final_g is not None:
        in_specs.append(pl.BlockSpec((1, D), lambda b, i: (0, 0)))
        args.append(final_g.reshape(1, D))
        kern = _outproj_final_kernel
    return pl.pallas_call(
        kern,
        grid=(B, T // tm),
        in_specs=in_specs,
        out_specs=tok(D),
        out_shape=jax.ShapeDtypeStruct((B, T, D), F32),
        compiler_params=_cparams(("parallel", "parallel")),
        name="out_proj",
    )(*args)


def _gelu_tanh(x):
    return 0.5 * x * (1.0 + jnp.tanh(math.sqrt(2.0 / math.pi) * (x + 0.044715 * (x * x * x))))


def _diag_blocks(r, lane_grp):
    n = r.shape[0] // 4
    out = r[0:n]
    for g in range(1, 4):
        out = jnp.where(lane_grp == g, r[g * n:(g + 1) * n], out)
    return out


def _local_kernel(pb_ref, pc_ref, prev_ref, next_ref, sgw_ref, sgb_ref, sgn_ref, band_ref, pw_ref, ps_ref,
                  yb_ref, yc_ref, *, T, tl):
    i = pl.program_id(1)
    nblk = T // tl
    n = SGU_CHUNK
    lane_grp = lax.broadcasted_iota(jnp.int32, (n, 256), 1) // 64
    row = lax.broadcasted_iota(jnp.int32, (n, 256), 0)
    lo_w = jnp.zeros((n, 256), jnp.int32)
    hi_w = jnp.zeros((n, 256), jnp.int32)
    for g, w in enumerate(POOL_WINDOWS):
        lo_w = jnp.where(lane_grp == g, w // 2, lo_w)
        hi_w = jnp.where(lane_grp == g, w - w // 2, hi_w)
    prev_blk = jnp.where(i > 0, prev_ref[0, :, 0:256], jnp.zeros((HALO, 256), BF16))
    next_blk = jnp.where(i < nblk - 1, next_ref[0, :, 0:256], jnp.zeros((HALO, 256), BF16))
    nch = tl // n
    for j in range(nch):
        r0 = j * n
        uvg = pb_ref[0, r0:r0 + n, :]
        gel = _gelu_tanh(uvg[:, 0:512].astype(F32))
        u = gel[:, 0:256]
        v = gel[:, 256:512]
        vn = v * lax.rsqrt(jnp.mean(v * v, axis=-1, keepdims=True) + EPS) * sgn_ref[...]
        sv = _diag_blocks(_dot(sgw_ref[...], vn.astype(BF16)), lane_grp) + sgb_ref[...]
        yb_ref[0, r0:r0 + n, :] = (u * sv * _silu(uvg[:, 512:768].astype(F32))).astype(BF16)
        xt = pc_ref[0, r0:r0 + n, 0:256]
        pv = prev_blk if j == 0 else pc_ref[0, r0 - HALO:r0, 0:256]
        nx = next_blk if j == nch - 1 else pc_ref[0, r0 + n:r0 + n + HALO, 0:256]
        xcat = jnp.concatenate([pv, xt, nx], axis=0)
        wsum = _diag_blocks(_dot(band_ref[...], xcat), lane_grp)
        t = i * tl + r0 + row
        cnt = (jnp.minimum(t + hi_w, T) - jnp.maximum(t - lo_w, 0)).astype(F32)
        pooled = wsum / cnt - xt.astype(F32)
        yc = _dot(pooled.astype(BF16), pw_ref[...]) * ps_ref[...]
        yc_ref[0, r0:r0 + n, :] = (yc * _silu(pc_ref[0, r0:r0 + n, 256:512].astype(F32))).astype(BF16)


def _local(pb, pc, sgw, sgb, sgn, band, pw_bd, pscale, tl):
    B, T, _ = pb.shape
    hb = tl // HALO
    nh = T // HALO
    const = lambda shape: pl.BlockSpec(shape, lambda b, i: (0,) * len(shape))
    return pl.pallas_call(
        functools.partial(_local_kernel, T=T, tl=tl),
        grid=(B, T // tl),
        in_specs=[pl.BlockSpec((1, tl, 768), lambda b, i: (b, i, 0)),
                  pl.BlockSpec((1, tl, 512), lambda b, i: (b, i, 0)),
                  pl.BlockSpec((1, HALO, 512), lambda b, i: (b, jnp.maximum(i * hb - 1, 0), 0)),
                  pl.BlockSpec((1, HALO, 512), lambda b, i: (b, jnp.minimum((i + 1) * hb, nh - 1), 0)),
                  const((4 * SGU_CHUNK, SGU_CHUNK)), const((SGU_CHUNK, 256)), const((1, 256)),
                  const((4 * SGU_CHUNK, SGU_CHUNK + 2 * HALO)), const((256, 256)), const((1, 256))],
        out_specs=[pl.BlockSpec((1, tl, 256), lambda b, i: (b, i, 0)),
                   pl.BlockSpec((1, tl, 256), lambda b, i: (b, i, 0))],
        out_shape=[jax.ShapeDtypeStruct((B, T, 256), BF16), jax.ShapeDtypeStruct((B, T, 256), BF16)],
        compiler_params=_cparams(("parallel", "parallel")),
        name="sgu_pool",
    )(pb, pc, pc, pc, sgw, sgb, sgn, band, pw_bd, pscale)


def _dft_kernel(f_ref, z_ref, gd_ref, fw_ref, y_ref, *, inv_norm):
    f = _dot(f_ref[...], z_ref[0]) * inv_norm
    y = _dot(f.astype(BF16), fw_ref[...])
    y_ref[0] = (y * _silu(gd_ref[0].astype(F32))).astype(BF16)


def _fourier(z, gd, fmat, fw_bd, tq):
    B, T, _ = gd.shape
    z = z.reshape(B, 2 * T, 256)
    return pl.pallas_call(
        functools.partial(_dft_kernel, inv_norm=1.0 / math.sqrt(T * C_D)),
        grid=(T // tq, B),
        in_specs=[pl.BlockSpec((tq, 2 * T), lambda i, b: (i, 0)),
                  pl.BlockSpec((1, 2 * T, 256), lambda i, b: (b, 0, 0)),
                  pl.BlockSpec((1, tq, 256), lambda i, b: (b, i, 0)),
                  pl.BlockSpec((256, 256), lambda i, b: (0, 0))],
        out_specs=pl.BlockSpec((1, tq, 256), lambda i, b: (b, i, 0)),
        out_shape=jax.ShapeDtypeStruct((B, T, 256), BF16),
        compiler_params=_cparams(("parallel", "parallel")),
        name="dft_positions",
    )(fmat, z, gd, fw_bd)


TRI_UNITS = 128
TRI_STRIDE = DN_CHUNK + 8


def _expand(x, bd):
    return jnp.concatenate([x, x, x, x], axis=0) * bd


def _delta_prep_kernel(qkv_ref, prev_ref, next_ref, ba_ref, cw_ref, par_ref, sel_ref, grp_ref, tri_ref,
                       d0_ref, d1_ref, a_ref, gl_ref, *, T, tc):
    j = pl.program_id(1)
    nblk = T // tc
    x = qkv_ref[0].astype(F32)
    prev = jnp.where(j > 0, prev_ref[0].astype(F32), 0.0)
    nxt = jnp.where(j < nblk - 1, next_ref[0].astype(F32), 0.0)
    xcat = jnp.concatenate([prev, x, nxt], axis=0)
    n = tc + 2 * HALO
    acc = None
    for tap in range(CONV_W):
        s = (CONV_W // 2 - tap) % n
        xs = xcat if s == 0 else pltpu.roll(xcat, s, axis=0)
        term = xs[HALO:HALO + tc] * cw_ref[tap:tap + 1, :]
        acc = term if acc is None else acc + term
    c = _silu(acc)
    cq, ck, v = c[:, 0:256], c[:, 256:512], c[:, 512:768]
    grp = grp_ref[...]
    q = cq * lax.rsqrt(_dot((cq * cq).astype(BF16), grp) + EPS) * (DK ** -0.5)
    k = ck * lax.rsqrt(_dot((ck * ck).astype(BF16), grp) + EPS)
    ba = ba_ref[0]
    z = ba + par_ref[1:2, :]
    softplus = jnp.maximum(z, 0.0) + jnp.log1p(jnp.exp(-jnp.abs(z)))
    lane = lax.broadcasted_iota(jnp.int32, ba.shape, 1)
    bg = jnp.where(lane < 2 * H_A, jax.nn.sigmoid(ba), -jnp.exp(par_ref[0:1, :]) * softplus)
    gsum_f = _dot_exact_lhs(tri_ref[0], bg)
    gsum_b = _dot_exact_lhs(tri_ref[1], bg)
    narrow = jnp.where(lane < 2 * H_A, bg, jnp.where(lane < 3 * H_A, gsum_f, gsum_b))
    bgx = _dot_exact_rhs(narrow, sel_ref[...])
    gcum = (bgx[:, 512:768], bgx[:, 768:1024])

    bd = grp
    r64 = lax.broadcasted_iota(jnp.int32, (DN_CHUNK, 256), 0)
    c64 = lax.broadcasted_iota(jnp.int32, (DN_CHUNK, 256), 1) % 64
    diag64 = r64 == c64
    pad_rows = jnp.zeros((TRI_STRIDE - DN_CHUNK, 128), F32)
    out_refs = (d0_ref, d1_ref)
    for ch in range(tc // DN_CHUNK):
        r0 = ch * DN_CHUNK
        a0 = ch * TRI_STRIDE
        kx = k[r0:r0 + DN_CHUNK]
        qx = q[r0:r0 + DN_CHUNK]
        vx = v[r0:r0 + DN_CHUNK]
        kb = kx.astype(BF16)
        kq = _dot_nt(jnp.concatenate([kb, qx.astype(BF16)], axis=0), _expand(kb, bd))
        kk, qk = kq[0:DN_CHUNK], kq[DN_CHUNK:2 * DN_CHUNK]
        for d in range(2):
            gc = gcum[d][r0:r0 + DN_CHUNK]
            beta = bgx[r0:r0 + DN_CHUNK, d * 256:(d + 1) * 256]
            rowv = jnp.sum(jnp.where(diag64, gc, 0.0), axis=0, keepdims=True)
            incl = (r64 >= c64) if d == 0 else (r64 <= c64)
            strict = (r64 > c64) if d == 0 else (r64 < c64)
            decay = jnp.where(incl, jnp.exp(jnp.where(incl, gc - rowv, 0.0)), 0.0)
            a_c = jnp.where(strict, beta * kk * decay, 0.0)
            for half in range(2):
                a_ref[d, half, a0:a0 + DN_CHUNK, :] = a_c[:, 128 * half:128 * half + 128]
                a_ref[d, half, a0 + DN_CHUNK:a0 + TRI_STRIDE, :] = pad_rows
            egc = jnp.exp(gc)
            g_last = gc[DN_CHUNK - 1:DN_CHUNK] if d == 0 else gc[0:1]
            o = out_refs[d]
            o[0, r0:r0 + DN_CHUNK, 0:256] = (vx * beta).astype(BF16)
            o[0, r0:r0 + DN_CHUNK, 256:512] = (kx * beta * egc).astype(BF16)
            o[0, r0:r0 + DN_CHUNK, 512:768] = (qx * egc).astype(BF16)
            o[0, r0:r0 + DN_CHUNK, 768:1024] = (kx * jnp.exp(g_last - gc)).astype(BF16)
            o[0, r0:r0 + DN_CHUNK, 1024:1280] = (qk * decay).astype(BF16)
            gl_ref[0, d, ch] = jnp.exp(g_last)


def _delta_prep(pa, ba, conv_w, par, sel, grp, tri, tc):
    B, T, _ = pa.shape
    hb = tc // HALO
    nh = T // HALO
    nt = T // tc
    nch = tc // DN_CHUNK
    const = lambda shape: pl.BlockSpec(shape, lambda b, i: (0,) * len(shape))
    return pl.pallas_call(
        functools.partial(_delta_prep_kernel, T=T, tc=tc),
        grid=(B, nt),
        in_specs=[pl.BlockSpec((1, tc, 768), lambda b, i: (b, i, 0)),
                  pl.BlockSpec((1, HALO, 768), lambda b, i: (b, jnp.maximum(i * hb - 1, 0), 0)),
                  pl.BlockSpec((1, HALO, 768), lambda b, i: (b, jnp.minimum((i + 1) * hb, nh - 1), 0)),
                  pl.BlockSpec((1, tc, 128), lambda b, i: (b, i, 0)),
                  const((CONV_W, 768)), const((2, 128)), const((128, 1024)), const((256, 256)),
                  const((2, tc, tc))],
        out_specs=[pl.BlockSpec((1, tc, DREC), lambda b, i: (b, i, 0)),
                   pl.BlockSpec((1, tc, DREC), lambda b, i: (b, i, 0)),
                   pl.BlockSpec((2, 2, nch * TRI_STRIDE, 128), lambda b, i: (0, 0, b * nt + i, 0)),
                   pl.BlockSpec((1, 2, nch, 1, 256), lambda b, i: (b, 0, i, 0, 0))],
        out_shape=[jax.ShapeDtypeStruct((B, T, DREC), BF16),
                   jax.ShapeDtypeStruct((B, T, DREC), BF16),
                   jax.ShapeDtypeStruct((2, 2, B * nt * nch * TRI_STRIDE, 128), F32),
                   jax.ShapeDtypeStruct((B, 2, T // DN_CHUNK, 1, 256), F32)],
        compiler_params=_cparams(("parallel", "parallel")),
        name="delta_prep",
    )(pa, pa, pa, ba, conv_w, par, sel, grp, tri)


TRI_ROWS = TRI_UNITS * TRI_STRIDE


def _tri_rows(x_ref, lower):
    sub = lax.broadcasted_iota(jnp.int32, (8, TRI_UNITS), 0)
    nb = DN_CHUNK // 8
    for ib in (range(nb) if lower else reversed(range(nb))):
        segs = range(ib + 1) if lower else range(ib, nb)
        live = range(ib + 1) if lower else range(ib, nb)

        def row(step, carry, ib=ib, segs=segs, live=live):
            i = 8 * ib + (step if lower else 7 - step)
            for heads in ((0, 1), (2, 3)):
                acc = {(h, cv): jnp.zeros((8, TRI_UNITS), F32) for h in heads for cv in live}
                for jb in segs:
                    cvs = range(jb + 1) if lower else range(jb, nb)
                    for j in range(8 * jb, 8 * jb + 8):
                        for h in heads:
                            coef = jnp.broadcast_to(x_ref[i, pl.ds(h * DK + j, 1), :], (8, TRI_UNITS))
                            for cv in cvs:
                                acc[h, cv] = acc[h, cv] + coef * x_ref[j, h * DK + 8 * cv:h * DK + 8 * cv + 8, :]
                for h in heads:
                    for cv in range(nb):
                        val = (jnp.where(sub + 8 * cv == i, 1.0, -acc[h, cv]) if cv in live
                               else jnp.zeros((8, TRI_UNITS), F32))
                        x_ref[i, h * DK + 8 * cv:h * DK + 8 * cv + 8, :] = val
            return carry

        lax.fori_loop(0, 8, row, 0)


def _tri_inverse_kernel(a_ref, t_ref, x_ref):
    d = pl.program_id(0)

    def load(i, carry):
        for half in range(2):
            x_ref[i, 128 * half:128 * half + 128, :] = a_ref[0, half, pl.ds(i, TRI_UNITS, stride=TRI_STRIDE), :].T
        return carry

    lax.fori_loop(0, DN_CHUNK, load, 0, unroll=4)

    @pl.when(d == 0)
    def _():
        _tri_rows(x_ref, True)

    @pl.when(d == 1)
    def _():
        _tri_rows(x_ref, False)

    def store(i, carry):
        for half in range(2):
            t_ref[0, half, pl.ds(i, TRI_UNITS, stride=TRI_STRIDE), :] = x_ref[i, 128 * half:128 * half + 128, :].T
        return carry

    lax.fori_loop(0, DN_CHUNK, store, 0, unroll=4)
    for half in range(2):
        for p in range(DN_CHUNK, TRI_STRIDE):
            t_ref[0, half, pl.ds(p, TRI_UNITS, stride=TRI_STRIDE), :] = jnp.zeros((TRI_UNITS, 128), F32)


def _tri_inverse(a):
    rows = a.shape[2]
    spec = pl.BlockSpec((1, 2, TRI_ROWS, 128), lambda d, i: (d, 0, i, 0))
    return pl.pallas_call(
        _tri_inverse_kernel,
        grid=(2, rows // TRI_ROWS),
        in_specs=[spec],
        out_specs=spec,
        out_shape=jax.ShapeDtypeStruct(a.shape, F32),
        scratch_shapes=[pltpu.VMEM((DN_CHUNK, 256, TRI_UNITS), F32)],
        compiler_params=_cparams(("parallel", "parallel")),
        name="tri_inverse",
    )(a)


def _delta_scan_kernel_with_state(d0_ref, d1_ref, tf_ref, tb_ref, glf_ref, glb_ref, grp_ref, s0_ref, of_ref, ob_ref,
                                  sfin_ref, s_ref, *, tc, nb):
    _delta_scan_kernel(s0_ref, d0_ref, d1_ref, tf_ref, tb_ref, glf_ref, glb_ref, grp_ref, of_ref, ob_ref, sfin_ref,
                       s_ref, tc=tc, nb=nb)


def _delta_scan_kernel(s0_ref, d0_ref, d1_ref, tf_ref, tb_ref, glf_ref, glb_ref, grp_ref, of_ref, ob_ref, sfin_ref,
                       s_ref, *, tc, nb):
    j = pl.program_id(1)
    nblk = pl.num_programs(1)
    nch = tc // DN_CHUNK

    @pl.when(j == 0)
    def _():
        if s0_ref is None:
            s_ref[...] = jnp.zeros(s_ref.shape, F32)
        else:
            zero = jnp.zeros((DK, DK), F32)
            for q in range(nb):
                for d in range(2):
                    s_ref[q, d] = jnp.concatenate(
                        [jnp.concatenate([s0_ref[q, d, h] if g == h else zero for g in range(H_A)], axis=1)
                         for h in range(H_A)], axis=0)

    bd = grp_ref[...]
    bd_f32 = bd.astype(F32)
    rec = (d0_ref, d1_ref)
    tri = (tf_ref, tb_ref)
    glr = (glf_ref, glb_ref)
    out = (of_ref, ob_ref)
    chains = [(q, d) for q in range(nb) for d in range(2)]

    def chunk_of(d, step):
        return step if d == 0 else nch - 1 - step

    def rows_of(c):
        return slice(c * DN_CHUNK, (c + 1) * DN_CHUNK)

    uw = {}
    for step in range(nch):
        for q, d in chains:
            c = chunk_of(d, step)
            trows = slice(c * TRI_STRIDE, c * TRI_STRIDE + DN_CHUNK)
            tinv = jnp.concatenate([tri[d][0, 0, q, trows, :], tri[d][0, 1, q, trows, :]], axis=1).astype(BF16)
            rhs = jnp.concatenate([_expand(rec[d][q, rows_of(c), 0:256], bd),
                                   _expand(rec[d][q, rows_of(c), 256:512], bd)], axis=1)
            uw[q, d, step] = _dot(tinv, rhs)

    for step in range(nch):
        s = {k: s_ref[k[0], k[1]] for k in chains}
        ws_qs = {}
        for q, d in chains:
            r = rows_of(chunk_of(d, step))
            wq = jnp.concatenate([uw[q, d, step][:, 256:512].astype(BF16), rec[d][q, r, 512:768]], axis=0)
            ws_qs[q, d] = _dot(wq, s[q, d].astype(BF16))
        v_new = {k: (uw[k[0], k[1], step][:, 0:256] - ws_qs[k][0:DN_CHUNK]).astype(BF16) for k in chains}
        for q, d in chains:
            c = chunk_of(d, step)
            r = rows_of(c)
            out[d][q, r, :] = ws_qs[q, d][DN_CHUNK:2 * DN_CHUNK] + _dot(rec[d][q, r, 1024:1280], _expand(v_new[q, d], bd))
            upd = _dot_tn(rec[d][q, r, 768:1024], v_new[q, d])
            s_ref[q, d] = s[q, d] * glr[d][q, 0, c] + upd * bd_f32

    @pl.when(j == nblk - 1)
    def _():
        for q in range(nb):
            for d in range(2):
                s = s_ref[q, d]
                for h in range(H_A):
                    sfin_ref[q, d, h] = s[h * DK:(h + 1) * DK, h * DK:(h + 1) * DK]


def _delta_scan(d0, d1, tinv, gl, grp, s0, tc, nb):
    B, T, _ = d0.shape
    nblk = T // tc
    nch = tc // DN_CHUNK
    tinv = tinv.reshape(2, 2, B, nblk * nch * TRI_STRIDE, 128)
    state_spec = pl.BlockSpec((nb, 2, H_A, DK, DK), lambda b, j: (b, 0, 0, 0, 0))
    in_specs = [pl.BlockSpec((nb, tc, DREC), lambda b, j: (b, j, 0)),
                pl.BlockSpec((nb, tc, DREC), lambda b, j: (b, nblk - 1 - j, 0)),
                pl.BlockSpec((1, 2, nb, nch * TRI_STRIDE, 128), lambda b, j: (0, 0, b, j, 0)),
                pl.BlockSpec((1, 2, nb, nch * TRI_STRIDE, 128), lambda b, j: (1, 0, b, nblk - 1 - j, 0)),
                pl.BlockSpec((nb, 1, nch, 1, 256), lambda b, j: (b, 0, j, 0, 0)),
                pl.BlockSpec((nb, 1, nch, 1, 256), lambda b, j: (b, 1, nblk - 1 - j, 0, 0)),
                pl.BlockSpec((256, 256), lambda b, j: (0, 0))]
    args = [d0, d1, tinv, tinv, gl, gl, grp]
    if s0 is None:
        kern = functools.partial(_delta_scan_kernel, None, tc=tc, nb=nb)
    else:
        kern = functools.partial(_delta_scan_kernel_with_state, tc=tc, nb=nb)
        in_specs.append(state_spec)
        args.append(s0)
    return pl.pallas_call(
        kern,
        grid=(B // nb, nblk),
        in_specs=in_specs,
        out_specs=[pl.BlockSpec((nb, tc, 256), lambda b, j: (b, j, 0)),
                   pl.BlockSpec((nb, tc, 256), lambda b, j: (b, nblk - 1 - j, 0)),
                   state_spec],
        out_shape=[jax.ShapeDtypeStruct((B, T, 256), F32),
                   jax.ShapeDtypeStruct((B, T, 256), F32),
                   jax.ShapeDtypeStruct((B, 2, H_A, DK, DK), F32)],
        scratch_shapes=[pltpu.VMEM((nb, 2, 256, 256), F32)],
        compiler_params=_cparams(("parallel", "arbitrary")),
        name="delta_scan",
    )(*args)


def _block_diag4(w):
    n = w.shape[-1]
    eye = jnp.eye(4, dtype=w.dtype)
    return jnp.einsum("gij,gh->gihj", w, eye).reshape(4 * n, 4 * n)


def _dft_matrix(T):
    w = 2.0 * math.pi / T
    t = jnp.arange(T, dtype=jnp.int32)[None, None, :]
    a = jnp.arange(T // 128, dtype=jnp.int32)[:, None, None]
    b = jnp.arange(128, dtype=jnp.int32)[None, :, None]
    ang_a = ((128 * a * t) % T).astype(F32) * w
    ang_b = ((b * t) % T).astype(F32) * w
    ca, sa, cb, sb = jnp.cos(ang_a), jnp.sin(ang_a), jnp.cos(ang_b), jnp.sin(ang_b)
    cos_m = (ca * cb - sa * sb).reshape(T, T)
    sin_m = (sa * cb + ca * sb).reshape(T, T)
    return jnp.concatenate([cos_m, sin_m], axis=1).astype(BF16)


def _channel_dft():
    c = jnp.arange(C_D, dtype=jnp.int32)
    ang = ((c[:, None] * c[None, :]) % C_D).astype(F32) * (2.0 * math.pi / C_D)
    cos4 = _block_diag4(jnp.broadcast_to(jnp.cos(ang), (G_D, C_D, C_D)))
    sin4 = _block_diag4(jnp.broadcast_to(jnp.sin(ang), (G_D, C_D, C_D)))
    return jnp.concatenate([cos4, -sin4], axis=1).astype(BF16)


def _pool_band():
    n = SGU_CHUNK
    t = jnp.arange(n)[:, None]
    s = jnp.arange(n + 2 * HALO)[None, :] - HALO
    blocks = [((s >= t - w // 2) & (s < t + w - w // 2)) for w in POOL_WINDOWS]
    return jnp.concatenate(blocks, axis=0).astype(BF16)


def _chunk_triangles(tc):
    r = jnp.arange(tc)[:, None]
    c = jnp.arange(tc)[None, :]
    same = (r // DN_CHUNK) == (c // DN_CHUNK)
    return jnp.stack([same & (r >= c), same & (r <= c)]).astype(BF16)


def _gate_selector():
    src = jnp.arange(128)[:, None]
    dst = jnp.arange(1024)[None, :]
    kind, d, h = dst // 512, (dst // 256) % 2, (dst // 64) % 4
    return (src == kind * 8 + d * 4 + h).astype(BF16)


def _grid_pos_embed(T):
    rows = T // GRID_W
    r = jnp.repeat(jnp.arange(rows, dtype=F32), GRID_W)
    col = jnp.tile(jnp.arange(GRID_W, dtype=F32), rows)
    n_freq = D_MODEL // 4
    freqs = jnp.power(POS_BASE, -jnp.arange(n_freq, dtype=F32) / n_freq)
    ar = r[:, None] * freqs[None]
    ac = col[:, None] * freqs[None]
    return jnp.concatenate([jnp.sin(ar), jnp.cos(ar), jnp.sin(ac), jnp.cos(ac)], axis=-1)


def _layer_consts(l, w, wz):
    win = w["w_in"][l]
    w_in_p = jnp.concatenate([win[:, 0:1024], win[:, 1040:2320], win[:, 2576:2832], wz[l], win[:, 1024:1040],
                              jnp.zeros((D_MODEL, 128 - 4 * H_A), F32)], axis=1).astype(BF16)
    par = jnp.zeros((2, 128), F32)
    par = par.at[0, 8:16].set(w["a_log"][l].reshape(8)).at[1, 8:16].set(w["dt_bias"][l].reshape(8))
    return dict(
        norm_g=w["norm_g"][l], w_in_p=w_in_p, conv_w=w["conv_qkv"][l], par=par,
        dn_g4=jnp.tile(w["dn_norm_g"][l], H_A).reshape(1, 256),
        sgw=w["sgu_w"][l].reshape(G_B * SGU_CHUNK, SGU_CHUNK).astype(BF16),
        sgb=jnp.repeat(w["sgu_b"][l].T, W_B // G_B, axis=1),
        sgn=w["sgu_norm_g"][l].reshape(1, W_B),
        pw_bd=_block_diag4(w["pool_w"][l]).astype(BF16),
        pscale=w["pool_scale"][l].reshape(1, W_C),
        fw_bd=_block_diag4(w["fourier_w"][l]).astype(BF16),
        w_out=w["w_out"][l].astype(BF16),
    )


def _trunk_layer(x, pe, mod, s0, lc, shared, fmat, final_g):
    B, T, _ = x.shape
    tm = min(T, 512)
    outs = _inproj(x, pe, mod, lc["norm_g"], lc["w_in_p"], tm)
    if pe is not None:
        x, outs = outs[0], outs[1:]
    pa, pb, pc, gd, z, ba = outs
    d0, d1, a_tri, gl = _delta_prep(pa, ba, lc["conv_w"], lc["par"], shared["sel"], shared["grp"], shared["tri"],
                                    256)
    o_f, o_b, s_fin = _delta_scan(d0, d1, _tri_inverse(a_tri), gl, shared["grp"], s0, 256, SCAN_SEQS)
    yb, yc = _local(pb, pc, lc["sgw"], lc["sgb"], lc["sgn"], shared["band"], lc["pw_bd"], lc["pscale"], tm)
    yd = _fourier(z, gd, fmat, lc["fw_bd"], tm)
    xn = _outproj(o_f, o_b, pa, shared["grp"], lc["dn_g4"], yb, yc, yd, x, mod, lc["w_out"], final_g, tm)
    return xn, s_fin


def kernel(x_prompt, x_sample, state_delta, c, c_ctx, ada_w, ada_b, norm_g, w_in, conv_qkv, a_log, dt_bias,
           dn_norm_g, sgu_norm_g, sgu_w, sgu_b, pool_w, pool_scale, fourier_w, w_out, final_norm_g):
    w = dict(norm_g=norm_g, w_in=w_in, conv_qkv=conv_qkv, a_log=a_log, dt_bias=dt_bias, dn_norm_g=dn_norm_g,
             sgu_norm_g=sgu_norm_g, sgu_w=sgu_w, sgu_b=sgu_b, pool_w=pool_w, pool_scale=pool_scale,
             fourier_w=fourier_w, w_out=w_out)
    bp, tp, _ = x_prompt.shape
    bs, ts, _ = x_sample.shape
    conds = jnp.concatenate([c, c_ctx[None], jnp.zeros((16 - bs - 1, D_MODEL), F32)], axis=0)
    mod = _modulation(conds, ada_w, ada_b)
    ones64 = jnp.ones((H_A, DK, DK), F32)
    shared = dict(sel=_gate_selector(), grp=_block_diag4(ones64).astype(BF16), band=_pool_band(),
                  tri=_chunk_triangles(256))
    fmat_p = _dft_matrix(tp)
    fmat_s = _dft_matrix(ts)
    wz = _fold_channel_dft(w_in[:, :, 2320:2576], _channel_dft())
    layers = [_layer_consts(l, w, wz) for l in range(DEPTH)]

    xp, xs = x_prompt, x_sample
    ctx_states = []
    for l in range(DEPTH):
        fg = final_norm_g if l == DEPTH - 1 else None
        xp, s_l = _trunk_layer(xp, None, mod[l, bs:bs + 1].reshape(1, 1, 3 * D_MODEL), None, layers[l], shared,
                               fmat_p, fg)
        ctx_states.append(s_l)
        pe = _grid_pos_embed(ts) if l == 0 else None
        xs, _ = _trunk_layer(xs, pe, mod[l, 0:bs].reshape(bs, 1, 3 * D_MODEL), state_delta[:, l],
                             layers[l], shared, fmat_s, fg)
    return xp, xs, jnp.stack(ctx_states, axis=1)
```

```python
import functools
import math

import numpy as np
import jax
import jax.numpy as jnp
from jax import lax
from jax.experimental import pallas as pl
from jax.experimental.pallas import tpu as pltpu

F32 = jnp.float32
BF16 = jnp.bfloat16

D_MODEL = 1024
DEPTH = 4
GRID_W = 64
POS_BASE = 10000.0
H_A = 4
DK = 64
W_A = 256
CONV_W = 4
DN_CHUNK = 64
W_B = 256
G_B = 4
SGU_CHUNK = 128
W_C = 256
POOL_WINDOWS = (2, 4, 8, 16)
W_D = 256
G_D = 4
C_D = 64
EPS = 1e-6

P_MAIN = 3072
P_PAD = P_MAIN + 128
HALO = 16
DREC = 5 * W_A
SCAN_SEQS = 4
DELTA_TILE = 256

VMEM_LIMIT = 56 * 1024 * 1024


def _cparams(sem):
    return pltpu.CompilerParams(dimension_semantics=sem, vmem_limit_bytes=VMEM_LIMIT)


def _silu(x):
    return x * jax.nn.sigmoid(x)


def _split2(x):
    hi = x.astype(BF16)
    lo = (x - hi.astype(F32)).astype(BF16)
    return hi, lo


def _split3(x):
    hi = x.astype(BF16)
    r = x - hi.astype(F32)
    mid = r.astype(BF16)
    lo = (r - mid.astype(F32)).astype(BF16)
    return hi, mid, lo


def _dot(a, b):
    return jnp.dot(a, b, preferred_element_type=F32)


def _dot_nt(a, b):
    return lax.dot_general(a, b, (((1,), (1,)), ((), ())), preferred_element_type=F32)


def _dot_tn(a, b):
    return lax.dot_general(a, b, (((0,), (0,)), ((), ())), preferred_element_type=F32)


def _dot_exact_rhs(x, sel):
    hi, mid, lo = _split3(x)
    return _dot(hi, sel) + _dot(mid, sel) + _dot(lo, sel)


def _dot_exact_lhs(sel, x):
    hi, mid, lo = _split3(x)
    return _dot(sel, hi) + _dot(sel, mid) + _dot(sel, lo)


def _mod_kernel(c_ref, w_ref, b_ref, o_ref):
    a = _silu(c_ref[...]).astype(BF16)
    o_ref[0] = _dot(a, w_ref[0].astype(BF16)) + b_ref[0]


def _modulation(conds, ada_w, ada_b):
    nc = conds.shape[0]
    tn = 1024
    return pl.pallas_call(
        _mod_kernel,
        grid=(DEPTH, 3 * D_MODEL // tn),
        in_specs=[pl.BlockSpec((nc, D_MODEL), lambda l, n: (0, 0)),
                  pl.BlockSpec((1, D_MODEL, tn), lambda l, n: (l, 0, n)),
                  pl.BlockSpec((1, 1, tn), lambda l, n: (l, 0, n))],
        out_specs=pl.BlockSpec((1, nc, tn), lambda l, n: (l, 0, n)),
        out_shape=jax.ShapeDtypeStruct((DEPTH, nc, 3 * D_MODEL), F32),
        compiler_params=_cparams(("parallel", "parallel")),
        name="adaln_mod",
    )(conds, ada_w, ada_b.reshape(DEPTH, 1, 3 * D_MODEL))


def _inproj_body(x, mod_ref, ng_ref, w_ref, pa_ref, pb_ref, pc_ref, gd_ref, z_ref, ba_ref):
    ms = jnp.mean(x * x, axis=-1, keepdims=True)
    y = x * lax.rsqrt(ms + EPS) * ng_ref[...]
    shift = mod_ref[0, :, 0:D_MODEL]
    scale = mod_ref[0, :, D_MODEL:2 * D_MODEL]
    h = (y * (1.0 + scale) + shift).astype(BF16)
    pa_ref[0] = _dot(h, w_ref[:, 0:1024]).astype(BF16)
    pb_ref[0] = _dot(h, w_ref[:, 1024:1792]).astype(BF16)
    pc_ref[0] = _dot(h, w_ref[:, 1792:2304]).astype(BF16)
    gd_ref[0] = _dot(h, w_ref[:, 2304:2560]).astype(BF16)
    z = _dot(h, w_ref[:, 2560:3072])
    z_ref[0, 0] = z[:, 0:256].astype(BF16)
    z_ref[0, 1] = z[:, 256:512].astype(BF16)
    ba_ref[0] = _dot(h, w_ref[:, 3072:3200])


def _inproj_kernel(x_ref, mod_ref, ng_ref, w_ref, pa_ref, pb_ref, pc_ref, gd_ref, z_ref, ba_ref):
    _inproj_body(x_ref[0], mod_ref, ng_ref, w_ref, pa_ref, pb_ref, pc_ref, gd_ref, z_ref, ba_ref)


def _inproj_pos_kernel(x_ref, pe_ref, mod_ref, ng_ref, w_ref, x0_ref, pa_ref, pb_ref, pc_ref, gd_ref, z_ref, ba_ref):
    x = x_ref[0] + pe_ref[...]
    x0_ref[0] = x
    _inproj_body(x, mod_ref, ng_ref, w_ref, pa_ref, pb_ref, pc_ref, gd_ref, z_ref, ba_ref)


def _inproj(x, pe, mod, norm_g, w_in_p, tm):
    B, T, D = x.shape
    nmod = mod.shape[0]
    mod_map = (lambda b, i: (b, 0, 0)) if nmod == B else (lambda b, i: (0, 0, 0))
    tok = lambda n: pl.BlockSpec((1, tm, n), lambda b, i: (b, i, 0))
    in_specs = [pl.BlockSpec((1, 1, 3 * D), mod_map),
                pl.BlockSpec((1, D), lambda b, i: (0, 0)),
                pl.BlockSpec((D, P_PAD), lambda b, i: (0, 0))]
    out_specs = [tok(1024), tok(768), tok(512), tok(256),
                 pl.BlockSpec((1, 2, tm, 256), lambda b, i: (b, 0, i, 0)), tok(128)]
    out_shape = [jax.ShapeDtypeStruct((B, T, 1024), BF16),
                 jax.ShapeDtypeStruct((B, T, 768), BF16),
                 jax.ShapeDtypeStruct((B, T, 512), BF16),
                 jax.ShapeDtypeStruct((B, T, 256), BF16),
                 jax.ShapeDtypeStruct((B, 2, T, 256), BF16),
                 jax.ShapeDtypeStruct((B, T, 128), F32)]
    args = [mod, norm_g.reshape(1, D), w_in_p]
    if pe is None:
        kern, in_specs, args = _inproj_kernel, [tok(D)] + in_specs, [x] + args
    else:
        kern = _inproj_pos_kernel
        in_specs = [tok(D), pl.BlockSpec((tm, D), lambda b, i: (i, 0))] + in_specs
        args = [x, pe] + args
        out_specs = [tok(D)] + out_specs
        out_shape = [jax.ShapeDtypeStruct((B, T, D), F32)] + out_shape
    return pl.pallas_call(
        kern,
        grid=(B, T // tm),
        in_specs=in_specs,
        out_specs=out_specs,
        out_shape=out_shape,
        compiler_params=_cparams(("parallel", "parallel")),
        name="in_proj",
    )(*args)


def _fold_kernel(w_ref, cs_ref, o_ref):
    o_ref[0] = _dot_exact_rhs(w_ref[0], cs_ref[...])


def _fold_channel_dft(w_xd, cs):
    return pl.pallas_call(
        _fold_kernel,
        grid=(DEPTH,),
        in_specs=[pl.BlockSpec((1, D_MODEL, 256), lambda l: (l, 0, 0)),
                  pl.BlockSpec((256, 512), lambda l: (0, 0))],
        out_specs=pl.BlockSpec((1, D_MODEL, 512), lambda l: (l, 0, 0)),
        out_shape=jax.ShapeDtypeStruct((DEPTH, D_MODEL, 512), F32),
        compiler_params=_cparams(("parallel",)),
        name="fold_channel_dft",
    )(w_xd, cs)


def _outproj_body(of_ref, ob_ref, ga_ref, grp_ref, ng_ref, yb_ref, yc_ref, yd_ref, x_ref, mod_ref, w_ref):
    o = of_ref[0] + ob_ref[0]
    ms = _dot((o * o).astype(BF16), grp_ref[...]) * (1.0 / DK)
    ya = (o * lax.rsqrt(ms + EPS) * ng_ref[...] * _silu(ga_ref[0].astype(F32))).astype(BF16)
    ycat = jnp.concatenate([ya, yb_ref[0], yc_ref[0], yd_ref[0]], axis=-1)
    out = _dot(ycat, w_ref[...])
    gate = mod_ref[0, :, 2 * D_MODEL:3 * D_MODEL]
    return x_ref[0] + gate * out


def _outproj_kernel(of_ref, ob_ref, ga_ref, grp_ref, ng_ref, yb_ref, yc_ref, yd_ref, x_ref, mod_ref, w_ref, o_ref):
    o_ref[0] = _outproj_body(of_ref, ob_ref, ga_ref, grp_ref, ng_ref, yb_ref, yc_ref, yd_ref, x_ref, mod_ref, w_ref)


def _outproj_final_kernel(of_ref, ob_ref, ga_ref, grp_ref, ng_ref, yb_ref, yc_ref, yd_ref, x_ref, mod_ref, w_ref,
                          fg_ref, o_ref):
    xn = _outproj_body(of_ref, ob_ref, ga_ref, grp_ref, ng_ref, yb_ref, yc_ref, yd_ref, x_ref, mod_ref, w_ref)
    ms = jnp.mean(xn * xn, axis=-1, keepdims=True)
    o_ref[0] = xn * lax.rsqrt(ms + EPS) * fg_ref[...]


def _outproj(o_f, o_b, pa, grp, ng4, yb, yc, yd, x, mod, w_out, final_g, tm):
    B, T, D = x.shape
    nmod = mod.shape[0]
    mod_map = (lambda b, i: (b, 0, 0)) if nmod == B else (lambda b, i: (0, 0, 0))
    tok = lambda n: pl.BlockSpec((1, tm, n), lambda b, i: (b, i, 0))
    in_specs = [tok(256), tok(256),
                pl.BlockSpec((1, tm, 256), lambda b, i: (b, i, 3)),
                pl.BlockSpec((256, 256), lambda b, i: (0, 0)),
                pl.BlockSpec((1, 256), lambda b, i: (0, 0)),
                tok(256), tok(256), tok(256), tok(D),
                pl.BlockSpec((1, 1, 3 * D), mod_map),
                pl.BlockSpec((D, D), lambda b, i: (0, 0))]
    args = [o_f, o_b, pa, grp, ng4, yb, yc, yd, x, mod, w_out]
    kern = _outproj_kernel
    if final_g is not None:
        in_specs.append(pl.BlockSpec((1, D), lambda b, i: (0, 0)))
        args.append(final_g.reshape(1, D))
        kern = _outproj_final_kernel
    return pl.pallas_call(
        kern,
        grid=(B, T // tm),
        in_specs=in_specs,
        out_specs=tok(D),
        out_shape=jax.ShapeDtypeStruct((B, T, D), F32),
        compiler_params=_cparams(("parallel", "parallel")),
        name="out_proj",
    )(*args)


def _gelu_tanh(x):
    return 0.5 * x * (1.0 + jnp.tanh(math.sqrt(2.0 / math.pi) * (x + 0.044715 * (x * x * x))))


def _diag_blocks(r, lane_grp):
    n = r.shape[0] // 4
    out = r[0:n]
    for g in range(1, 4):
        out = jnp.where(lane_grp == g, r[g * n:(g + 1) * n], out)
    return out


def _local_kernel(pb_ref, pc_ref, prev_ref, next_ref, sgw_ref, sgb_ref, sgn_ref, band_ref, pw_ref, ps_ref,
                  yb_ref, yc_ref, *, T, tl):
    i = pl.program_id(1)
    nblk = T // tl
    n = SGU_CHUNK
    lane_grp = lax.broadcasted_iota(jnp.int32, (n, 256), 1) // 64
    row = lax.broadcasted_iota(jnp.int32, (n, 256), 0)
    lo_w = jnp.zeros((n, 256), jnp.int32)
    hi_w = jnp.zeros((n, 256), jnp.int32)
    for g, w in enumerate(POOL_WINDOWS):
        lo_w = jnp.where(lane_grp == g, w // 2, lo_w)
        hi_w = jnp.where(lane_grp == g, w - w // 2, hi_w)
    prev_blk = jnp.where(i > 0, prev_ref[0, :, 0:256], jnp.zeros((HALO, 256), BF16))
    next_blk = jnp.where(i < nblk - 1, next_ref[0, :, 0:256], jnp.zeros((HALO, 256), BF16))
    nch = tl // n
    for j in range(nch):
        r0 = j * n
        uvg = pb_ref[0, r0:r0 + n, :]
        gel = _gelu_tanh(uvg[:, 0:512].astype(F32))
        u = gel[:, 0:256]
        v = gel[:, 256:512]
        vn = v * lax.rsqrt(jnp.mean(v * v, axis=-1, keepdims=True) + EPS) * sgn_ref[...]
        sv = _diag_blocks(_dot(sgw_ref[...], vn.astype(BF16)), lane_grp) + sgb_ref[...]
        yb_ref[0, r0:r0 + n, :] = (u * sv * _silu(uvg[:, 512:768].astype(F32))).astype(BF16)
        xt = pc_ref[0, r0:r0 + n, 0:256]
        pv = prev_blk if j == 0 else pc_ref[0, r0 - HALO:r0, 0:256]
        nx = next_blk if j == nch - 1 else pc_ref[0, r0 + n:r0 + n + HALO, 0:256]
        xcat = jnp.concatenate([pv, xt, nx], axis=0)
        wsum = _diag_blocks(_dot(band_ref[...], xcat), lane_grp)
        t = i * tl + r0 + row
        cnt = (jnp.minimum(t + hi_w, T) - jnp.maximum(t - lo_w, 0)).astype(F32)
        pooled = wsum / cnt - xt.astype(F32)
        yc = _dot(pooled.astype(BF16), pw_ref[...]) * ps_ref[...]
        yc_ref[0, r0:r0 + n, :] = (yc * _silu(pc_ref[0, r0:r0 + n, 256:512].astype(F32))).astype(BF16)


def _local(pb, pc, sgw, sgb, sgn, band, pw_bd, pscale, tl):
    B, T, _ = pb.shape
    hb = tl // HALO
    nh = T // HALO
    const = lambda shape: pl.BlockSpec(shape, lambda b, i: (0,) * len(shape))
    return pl.pallas_call(
        functools.partial(_local_kernel, T=T, tl=tl),
        grid=(B, T // tl),
        in_specs=[pl.BlockSpec((1, tl, 768), lambda b, i: (b, i, 0)),
                  pl.BlockSpec((1, tl, 512), lambda b, i: (b, i, 0)),
                  pl.BlockSpec((1, HALO, 512), lambda b, i: (b, jnp.maximum(i * hb - 1, 0), 0)),
                  pl.BlockSpec((1, HALO, 512), lambda b, i: (b, jnp.minimum((i + 1) * hb, nh - 1), 0)),
                  const((4 * SGU_CHUNK, SGU_CHUNK)), const((SGU_CHUNK, 256)), const((1, 256)),
                  const((4 * SGU_CHUNK, SGU_CHUNK + 2 * HALO)), const((256, 256)), const((1, 256))],
        out_specs=[pl.BlockSpec((1, tl, 256), lambda b, i: (b, i, 0)),
                   pl.BlockSpec((1, tl, 256), lambda b, i: (b, i, 0))],
        out_shape=[jax.ShapeDtypeStruct((B, T, 256), BF16), jax.ShapeDtypeStruct((B, T, 256), BF16)],
        compiler_params=_cparams(("parallel", "parallel")),
        name="sgu_pool",
    )(pb, pc, pc, pc, sgw, sgb, sgn, band, pw_bd, pscale)


DFT_EXTRA = 8


def _dft_kernel(f_ref, z_ref, glo_ref, ghi_ref, rev_ref, fw_ref, ylo_ref, yhi_ref, *, T, th, inv_norm):
    p = _dot(f_ref[:, 0:T], z_ref[0, 0:T, :])
    q = _dot(f_ref[:, T:2 * T], z_ref[0, T:2 * T, :])
    f_lo = ((p + q) * inv_norm)[0:th].astype(BF16)
    f_hi = ((p - q) * inv_norm)[1:th + 1].astype(BF16)
    f_hi = _dot(rev_ref[...], f_hi).astype(BF16)
    ylo_ref[0] = (_dot(f_lo, fw_ref[...]) * _silu(glo_ref[0].astype(F32))).astype(BF16)
    yhi_ref[0] = (_dot(f_hi, fw_ref[...]) * _silu(ghi_ref[0].astype(F32))).astype(BF16)


def _fourier(z, gd, fmat, rev, fw_bd, th):
    B, T, _ = gd.shape
    nh = T // 2 // th
    z = z.reshape(B, 2 * T, 256)
    y_lo, y_hi = pl.pallas_call(
        functools.partial(_dft_kernel, T=T, th=th, inv_norm=1.0 / math.sqrt(T * C_D)),
        grid=(nh, B),
        in_specs=[pl.BlockSpec((th + DFT_EXTRA, 2 * T), lambda i, b: (i, 0)),
                  pl.BlockSpec((1, 2 * T, 256), lambda i, b: (b, 0, 0)),
                  pl.BlockSpec((1, th, 256), lambda i, b: (b, i, 0)),
                  pl.BlockSpec((1, th, 256), lambda i, b: (b, 2 * nh - 1 - i, 0)),
                  pl.BlockSpec((th, th), lambda i, b: (0, 0)),
                  pl.BlockSpec((256, 256), lambda i, b: (0, 0))],
        out_specs=[pl.BlockSpec((1, th, 256), lambda i, b: (b, i, 0)),
                   pl.BlockSpec((1, th, 256), lambda i, b: (b, nh - 1 - i, 0))],
        out_shape=[jax.ShapeDtypeStruct((B, T // 2, 256), BF16), jax.ShapeDtypeStruct((B, T // 2, 256), BF16)],
        compiler_params=_cparams(("parallel", "parallel")),
        name="dft_positions",
    )(fmat, z, gd, gd, rev, fw_bd)
    return jnp.concatenate([y_lo, y_hi], axis=1)


TRI_UNITS = 128
TRI_STRIDE = DN_CHUNK + 8


def _expand(x, bd):
    return jnp.concatenate([x, x, x, x], axis=0) * bd


def _delta_prep_kernel(qkv_ref, prev_ref, next_ref, ba_ref, cw_ref, par_ref, sel_ref, grp_ref, tri_ref,
                       d0_ref, d1_ref, a_ref, gl_ref, *, T, tc):
    j = pl.program_id(1)
    nblk = T // tc
    x = qkv_ref[0].astype(F32)
    prev = jnp.where(j > 0, prev_ref[0].astype(F32), 0.0)
    nxt = jnp.where(j < nblk - 1, next_ref[0].astype(F32), 0.0)
    xcat = jnp.concatenate([prev, x, nxt], axis=0)
    n = tc + 2 * HALO
    acc = None
    for tap in range(CONV_W):
        s = (CONV_W // 2 - tap) % n
        xs = xcat if s == 0 else pltpu.roll(xcat, s, axis=0)
        term = xs[HALO:HALO + tc] * cw_ref[tap:tap + 1, :]
        acc = term if acc is None else acc + term
    c = _silu(acc)
    cq, ck, v = c[:, 0:256], c[:, 256:512], c[:, 512:768]
    grp = grp_ref[...]
    q = cq * lax.rsqrt(_dot((cq * cq).astype(BF16), grp) + EPS) * (DK ** -0.5)
    k = ck * lax.rsqrt(_dot((ck * ck).astype(BF16), grp) + EPS)
    ba = ba_ref[0].T[0:4 * H_A]
    z = ba + par_ref[1]
    softplus = jnp.maximum(z, 0.0) + jnp.log1p(jnp.exp(-jnp.abs(z)))
    row = lax.broadcasted_iota(jnp.int32, ba.shape, 0)
    bg = jnp.where(row < 2 * H_A, jax.nn.sigmoid(ba), -jnp.exp(par_ref[0]) * softplus)
    gsum_f = _dot_exact_rhs(bg, tri_ref[1])
    gsum_b = _dot_exact_rhs(bg, tri_ref[0])
    narrow = jnp.where(row < 2 * H_A, bg, jnp.where(row < 3 * H_A, gsum_f, gsum_b))
    hi, mid, lo = _split3(narrow)
    sel = sel_ref[...]
    bgx = _dot_tn(hi, sel) + _dot_tn(mid, sel) + _dot_tn(lo, sel)
    gcum = (bgx[:, 512:768], bgx[:, 768:1024])

    bd = grp
    r64 = lax.broadcasted_iota(jnp.int32, (DN_CHUNK, 256), 0)
    c64 = lax.broadcasted_iota(jnp.int32, (DN_CHUNK, 256), 1) % 64
    diag64 = r64 == c64
    pad_rows = jnp.zeros((TRI_STRIDE - DN_CHUNK, 128), F32)
    out_refs = (d0_ref, d1_ref)
    for ch in range(tc // DN_CHUNK):
        r0 = ch * DN_CHUNK
        a0 = ch * TRI_STRIDE
        kx = k[r0:r0 + DN_CHUNK]
        qx = q[r0:r0 + DN_CHUNK]
        vx = v[r0:r0 + DN_CHUNK]
        kb = kx.astype(BF16)
        kq = _dot_nt(jnp.concatenate([kb, qx.astype(BF16)], axis=0), _expand(kb, bd))
        kk, qk = kq[0:DN_CHUNK], kq[DN_CHUNK:2 * DN_CHUNK]
        for d in range(2):
            gc = gcum[d][r0:r0 + DN_CHUNK]
            beta = bgx[r0:r0 + DN_CHUNK, d * 256:(d + 1) * 256]
            rowv = jnp.sum(jnp.where(diag64, gc, 0.0), axis=0, keepdims=True)
            incl = (r64 >= c64) if d == 0 else (r64 <= c64)
            strict = (r64 > c64) if d == 0 else (r64 < c64)
            decay = jnp.where(incl, jnp.exp(jnp.where(incl, gc - rowv, 0.0)), 0.0)
            a_c = jnp.where(strict, beta * kk * decay, 0.0)
            for half in range(2):
                a_ref[d, half, a0:a0 + DN_CHUNK, :] = a_c[:, 128 * half:128 * half + 128]
                a_ref[d, half, a0 + DN_CHUNK:a0 + TRI_STRIDE, :] = pad_rows
            egc = jnp.exp(gc)
            g_last = gc[DN_CHUNK - 1:DN_CHUNK] if d == 0 else gc[0:1]
            o = out_refs[d]
            o[0, r0:r0 + DN_CHUNK, 0:256] = (vx * beta).astype(BF16)
            o[0, r0:r0 + DN_CHUNK, 256:512] = (kx * beta * egc).astype(BF16)
            o[0, r0:r0 + DN_CHUNK, 512:768] = (qx * egc).astype(BF16)
            o[0, r0:r0 + DN_CHUNK, 768:1024] = (kx * jnp.exp(g_last - gc)).astype(BF16)
            o[0, r0:r0 + DN_CHUNK, 1024:1280] = (qk * decay).astype(BF16)
            gl_ref[0, d, ch] = jnp.exp(g_last)


def _delta_prep(pa, ba, conv_w, par, sel, grp, tri, tc):
    B, T, _ = pa.shape
    hb = tc // HALO
    nh = T // HALO
    nt = T // tc
    nch = tc // DN_CHUNK
    const = lambda shape: pl.BlockSpec(shape, lambda b, i: (0,) * len(shape))
    return pl.pallas_call(
        functools.partial(_delta_prep_kernel, T=T, tc=tc),
        grid=(B, nt),
        in_specs=[pl.BlockSpec((1, tc, 768), lambda b, i: (b, i, 0)),
                  pl.BlockSpec((1, HALO, 768), lambda b, i: (b, jnp.maximum(i * hb - 1, 0), 0)),
                  pl.BlockSpec((1, HALO, 768), lambda b, i: (b, jnp.minimum((i + 1) * hb, nh - 1), 0)),
                  pl.BlockSpec((1, tc, 128), lambda b, i: (b, i, 0)),
                  const((CONV_W, 768)), const((2, 4 * H_A, tc)), const((4 * H_A, 1024)), const((256, 256)),
                  const((2, tc, tc))],
        out_specs=[pl.BlockSpec((1, tc, DREC), lambda b, i: (b, i, 0)),
                   pl.BlockSpec((1, tc, DREC), lambda b, i: (b, i, 0)),
                   pl.BlockSpec((2, 2, nch * TRI_STRIDE, 128), lambda b, i: (0, 0, b * nt + i, 0)),
                   pl.BlockSpec((1, 2, nch, 1, 256), lambda b, i: (b, 0, i, 0, 0))],
        out_shape=[jax.ShapeDtypeStruct((B, T, DREC), BF16),
                   jax.ShapeDtypeStruct((B, T, DREC), BF16),
                   jax.ShapeDtypeStruct((2, 2, B * nt * nch * TRI_STRIDE, 128), F32),
                   jax.ShapeDtypeStruct((B, 2, T // DN_CHUNK, 1, 256), F32)],
        compiler_params=_cparams(("parallel", "parallel")),
        name="delta_prep",
    )(pa, pa, pa, ba, conv_w, par, sel, grp, tri)


TRI_ROWS = TRI_UNITS * TRI_STRIDE


def _tri_rows(x_ref, lower):
    sub = lax.broadcasted_iota(jnp.int32, (8, TRI_UNITS), 0)
    nb = DN_CHUNK // 8
    for ib in (range(nb) if lower else reversed(range(nb))):
        segs = range(ib + 1) if lower else range(ib, nb)
        live = range(ib + 1) if lower else range(ib, nb)

        def row(step, carry, ib=ib, segs=segs, live=live):
            i = 8 * ib + (step if lower else 7 - step)
            for heads in ((0, 1), (2, 3)):
                acc = {(h, cv): jnp.zeros((8, TRI_UNITS), F32) for h in heads for cv in live}
                for jb in segs:
                    cvs = range(jb + 1) if lower else range(jb, nb)
                    for j in range(8 * jb, 8 * jb + 8):
                        for h in heads:
                            coef = jnp.broadcast_to(x_ref[i, pl.ds(h * DK + j, 1), :], (8, TRI_UNITS))
                            for cv in cvs:
                                acc[h, cv] = acc[h, cv] + coef * x_ref[j, h * DK + 8 * cv:h * DK + 8 * cv + 8, :]
                for h in heads:
                    for cv in range(nb):
                        val = (jnp.where(sub + 8 * cv == i, 1.0, -acc[h, cv]) if cv in live
                               else jnp.zeros((8, TRI_UNITS), F32))
                        x_ref[i, h * DK + 8 * cv:h * DK + 8 * cv + 8, :] = val
            return carry

        lax.fori_loop(0, 8, row, 0)


def _tri_inverse_kernel(a_ref, t_ref, x_ref):
    d = pl.program_id(0)

    def load(i, carry):
        for half in range(2):
            x_ref[i, 128 * half:128 * half + 128, :] = a_ref[0, half, pl.ds(i, TRI_UNITS, stride=TRI_STRIDE), :].T
        return carry

    lax.fori_loop(0, DN_CHUNK, load, 0, unroll=4)

    @pl.when(d == 0)
    def _():
        _tri_rows(x_ref, True)

    @pl.when(d == 1)
    def _():
        _tri_rows(x_ref, False)

    def store(i, carry):
        for half in range(2):
            t_ref[0, half, pl.ds(i, TRI_UNITS, stride=TRI_STRIDE), :] = x_ref[i, 128 * half:128 * half + 128, :].T
        return carry

    lax.fori_loop(0, DN_CHUNK, store, 0, unroll=4)
    for half in range(2):
        for p in range(DN_CHUNK, TRI_STRIDE):
            t_ref[0, half, pl.ds(p, TRI_UNITS, stride=TRI_STRIDE), :] = jnp.zeros((TRI_UNITS, 128), F32)


def _tri_inverse(a):
    rows = a.shape[2]
    spec = pl.BlockSpec((1, 2, TRI_ROWS, 128), lambda d, i: (d, 0, i, 0))
    return pl.pallas_call(
        _tri_inverse_kernel,
        grid=(2, rows // TRI_ROWS),
        in_specs=[spec],
        out_specs=spec,
        out_shape=jax.ShapeDtypeStruct(a.shape, F32),
        scratch_shapes=[pltpu.VMEM((DN_CHUNK, 256, TRI_UNITS), F32)],
        compiler_params=_cparams(("parallel", "parallel")),
        name="tri_inverse",
    )(a)


def _delta_scan_kernel_with_state(d0_ref, d1_ref, tf_ref, tb_ref, glf_ref, glb_ref, grp_ref, s0_ref, of_ref, ob_ref,
                                  sfin_ref, s_ref, *, tc, nb):
    _delta_scan_kernel(s0_ref, d0_ref, d1_ref, tf_ref, tb_ref, glf_ref, glb_ref, grp_ref, of_ref, ob_ref, sfin_ref,
                       s_ref, tc=tc, nb=nb)


def _delta_scan_kernel(s0_ref, d0_ref, d1_ref, tf_ref, tb_ref, glf_ref, glb_ref, grp_ref, of_ref, ob_ref, sfin_ref,
                       s_ref, *, tc, nb):
    j = pl.program_id(1)
    nblk = pl.num_programs(1)
    nch = tc // DN_CHUNK

    @pl.when(j == 0)
    def _():
        if s0_ref is None:
            s_ref[...] = jnp.zeros(s_ref.shape, F32)
        else:
            zero = jnp.zeros((DK, DK), F32)
            for q in range(nb):
                for d in range(2):
                    s_ref[q, d] = jnp.concatenate(
                        [jnp.concatenate([s0_ref[q, d, h] if g == h else zero for g in range(H_A)], axis=1)
                         for h in range(H_A)], axis=0)

    bd = grp_ref[...]
    bd_f32 = bd.astype(F32)
    rec = (d0_ref, d1_ref)
    tri = (tf_ref, tb_ref)
    glr = (glf_ref, glb_ref)
    out = (of_ref, ob_ref)
    chains = [(q, d) for q in range(nb) for d in range(2)]

    def chunk_of(d, step):
        return step if d == 0 else nch - 1 - step

    def rows_of(c):
        return slice(c * DN_CHUNK, (c + 1) * DN_CHUNK)

    uw = {}
    for step in range(nch):
        for q, d in chains:
            c = chunk_of(d, step)
            trows = slice(c * TRI_STRIDE, c * TRI_STRIDE + DN_CHUNK)
            tinv = jnp.concatenate([tri[d][0, 0, q, trows, :], tri[d][0, 1, q, trows, :]], axis=1).astype(BF16)
            rhs = jnp.concatenate([_expand(rec[d][q, rows_of(c), 0:256], bd),
                                   _expand(rec[d][q, rows_of(c), 256:512], bd)], axis=1)
            uw[q, d, step] = _dot(tinv, rhs)

    for step in range(nch):
        s = {k: s_ref[k[0], k[1]] for k in chains}
        ws_qs = {}
        for q, d in chains:
            r = rows_of(chunk_of(d, step))
            wq = jnp.concatenate([uw[q, d, step][:, 256:512].astype(BF16), rec[d][q, r, 512:768]], axis=0)
            ws_qs[q, d] = _dot(wq, s[q, d].astype(BF16))
        v_new = {k: (uw[k[0], k[1], step][:, 0:256] - ws_qs[k][0:DN_CHUNK]).astype(BF16) for k in chains}
        for q, d in chains:
            c = chunk_of(d, step)
            r = rows_of(c)
            out[d][q, r, :] = ws_qs[q, d][DN_CHUNK:2 * DN_CHUNK] + _dot(rec[d][q, r, 1024:1280], _expand(v_new[q, d], bd))
            upd = _dot_tn(rec[d][q, r, 768:1024], v_new[q, d])
            s_ref[q, d] = s[q, d] * glr[d][q, 0, c] + upd * bd_f32

    @pl.when(j == nblk - 1)
    def _():
        for q in range(nb):
            for d in range(2):
                s = s_ref[q, d]
                for h in range(H_A):
                    sfin_ref[q, d, h] = s[h * DK:(h + 1) * DK, h * DK:(h + 1) * DK]


def _delta_scan(d0, d1, tinv, gl, grp, s0, tc, nb):
    B, T, _ = d0.shape
    nblk = T // tc
    nch = tc // DN_CHUNK
    tinv = tinv.reshape(2, 2, B, nblk * nch * TRI_STRIDE, 128)
    state_spec = pl.BlockSpec((nb, 2, H_A, DK, DK), lambda b, j: (b, 0, 0, 0, 0))
    in_specs = [pl.BlockSpec((nb, tc, DREC), lambda b, j: (b, j, 0)),
                pl.BlockSpec((nb, tc, DREC), lambda b, j: (b, nblk - 1 - j, 0)),
                pl.BlockSpec((1, 2, nb, nch * TRI_STRIDE, 128), lambda b, j: (0, 0, b, j, 0)),
                pl.BlockSpec((1, 2, nb, nch * TRI_STRIDE, 128), lambda b, j: (1, 0, b, nblk - 1 - j, 0)),
                pl.BlockSpec((nb, 1, nch, 1, 256), lambda b, j: (b, 0, j, 0, 0)),
                pl.BlockSpec((nb, 1, nch, 1, 256), lambda b, j: (b, 1, nblk - 1 - j, 0, 0)),
                pl.BlockSpec((256, 256), lambda b, j: (0, 0))]
    args = [d0, d1, tinv, tinv, gl, gl, grp]
    if s0 is None:
        kern = functools.partial(_delta_scan_kernel, None, tc=tc, nb=nb)
    else:
        kern = functools.partial(_delta_scan_kernel_with_state, tc=tc, nb=nb)
        in_specs.append(state_spec)
        args.append(s0)
    return pl.pallas_call(
        kern,
        grid=(B // nb, nblk),
        in_specs=in_specs,
        out_specs=[pl.BlockSpec((nb, tc, 256), lambda b, j: (b, j, 0)),
                   pl.BlockSpec((nb, tc, 256), lambda b, j: (b, nblk - 1 - j, 0)),
                   state_spec],
        out_shape=[jax.ShapeDtypeStruct((B, T, 256), F32),
                   jax.ShapeDtypeStruct((B, T, 256), F32),
                   jax.ShapeDtypeStruct((B, 2, H_A, DK, DK), F32)],
        scratch_shapes=[pltpu.VMEM((nb, 2, 256, 256), F32)],
        compiler_params=_cparams(("parallel", "arbitrary")),
        name="delta_scan",
    )(*args)


def _block_diag4(w):
    n = w.shape[-1]
    eye = jnp.eye(4, dtype=w.dtype)
    return jnp.einsum("gij,gh->gihj", w, eye).reshape(4 * n, 4 * n)


def _dft_matrix(T, th):
    w = 2.0 * math.pi / T
    nh = T // 2 // th
    groups = (th + DFT_EXTRA) // 8
    t = jnp.arange(T, dtype=jnp.int32)[None, None, :]
    coarse = (th * jnp.arange(nh, dtype=jnp.int32)[:, None] + 8 * jnp.arange(groups, dtype=jnp.int32)[None, :])
    coarse = coarse.reshape(nh * groups, 1, 1)
    fine = jnp.arange(8, dtype=jnp.int32)[None, :, None]
    ang_a = ((coarse * t) % T).astype(F32) * w
    ang_b = ((fine * t) % T).astype(F32) * w
    ca, sa, cb, sb = jnp.cos(ang_a), jnp.sin(ang_a), jnp.cos(ang_b), jnp.sin(ang_b)
    rows = nh * (th + DFT_EXTRA)
    cos_m = (ca * cb - sa * sb).reshape(rows, T)
    sin_m = (sa * cb + ca * sb).reshape(rows, T)
    return jnp.concatenate([cos_m, sin_m], axis=1).astype(BF16)


def _row_reversal(n):
    r = jnp.arange(n)
    return (r[:, None] + r[None, :] == n - 1).astype(BF16)


def _channel_dft():
    c = jnp.arange(C_D, dtype=jnp.int32)
    ang = ((c[:, None] * c[None, :]) % C_D).astype(F32) * (2.0 * math.pi / C_D)
    cos4 = _block_diag4(jnp.broadcast_to(jnp.cos(ang), (G_D, C_D, C_D)))
    sin4 = _block_diag4(jnp.broadcast_to(jnp.sin(ang), (G_D, C_D, C_D)))
    return jnp.concatenate([cos4, -sin4], axis=1).astype(BF16)


def _pool_band():
    n = SGU_CHUNK
    t = jnp.arange(n)[:, None]
    s = jnp.arange(n + 2 * HALO)[None, :] - HALO
    blocks = [((s >= t - w // 2) & (s < t + w - w // 2)) for w in POOL_WINDOWS]
    return jnp.concatenate(blocks, axis=0).astype(BF16)


def _chunk_triangles(tc):
    r = jnp.arange(tc)[:, None]
    c = jnp.arange(tc)[None, :]
    same = (r // DN_CHUNK) == (c // DN_CHUNK)
    return jnp.stack([same & (r >= c), same & (r <= c)]).astype(BF16)


def _gate_selector():
    src = jnp.arange(4 * H_A)[:, None]
    dst = jnp.arange(1024)[None, :]
    kind, d, h = dst // 512, (dst // 256) % 2, (dst // 64) % 4
    return (src == kind * 8 + d * 4 + h).astype(BF16)


def _grid_pos_embed(T):
    rows = T // GRID_W
    r = jnp.repeat(jnp.arange(rows, dtype=F32), GRID_W)
    col = jnp.tile(jnp.arange(GRID_W, dtype=F32), rows)
    n_freq = D_MODEL // 4
    freqs = jnp.power(POS_BASE, -jnp.arange(n_freq, dtype=F32) / n_freq)
    ar = r[:, None] * freqs[None]
    ac = col[:, None] * freqs[None]
    return jnp.concatenate([jnp.sin(ar), jnp.cos(ar), jnp.sin(ac), jnp.cos(ac)], axis=-1)


def _layer_consts(l, w, wz):
    win = w["w_in"][l]
    w_in_p = jnp.concatenate([win[:, 0:1024], win[:, 1040:2320], win[:, 2576:2832], wz[l], win[:, 1024:1040],
                              jnp.zeros((D_MODEL, 128 - 4 * H_A), F32)], axis=1).astype(BF16)
    par = jnp.stack([jnp.concatenate([jnp.zeros((2 * H_A,), F32), w[name][l].reshape(2 * H_A)])
                     for name in ("a_log", "dt_bias")])
    par = jnp.broadcast_to(par[:, :, None], (2, 4 * H_A, DELTA_TILE))
    return dict(
        norm_g=w["norm_g"][l], w_in_p=w_in_p, conv_w=w["conv_qkv"][l], par=par,
        dn_g4=jnp.tile(w["dn_norm_g"][l], H_A).reshape(1, 256),
        sgw=w["sgu_w"][l].reshape(G_B * SGU_CHUNK, SGU_CHUNK).astype(BF16),
        sgb=jnp.repeat(w["sgu_b"][l].T, W_B // G_B, axis=1),
        sgn=w["sgu_norm_g"][l].reshape(1, W_B),
        pw_bd=_block_diag4(w["pool_w"][l]).astype(BF16),
        pscale=w["pool_scale"][l].reshape(1, W_C),
        fw_bd=_block_diag4(w["fourier_w"][l]).astype(BF16),
        w_out=w["w_out"][l].astype(BF16),
    )


def _trunk_layer(x, pe, mod, s0, lc, shared, fmat, final_g):
    B, T, _ = x.shape
    tm = min(T, 512)
    outs = _inproj(x, pe, mod, lc["norm_g"], lc["w_in_p"], tm)
    if pe is not None:
        x, outs = outs[0], outs[1:]
    pa, pb, pc, gd, z, ba = outs
    d0, d1, a_tri, gl = _delta_prep(pa, ba, lc["conv_w"], lc["par"], shared["sel"], shared["grp"], shared["tri"],
                                    DELTA_TILE)
    o_f, o_b, s_fin = _delta_scan(d0, d1, _tri_inverse(a_tri), gl, shared["grp"], s0, DELTA_TILE, SCAN_SEQS)
    yb, yc = _local(pb, pc, lc["sgw"], lc["sgb"], lc["sgn"], shared["band"], lc["pw_bd"], lc["pscale"], tm)
    yd = _fourier(z, gd, fmat["cs"], fmat["rev"], lc["fw_bd"], fmat["th"])
    xn = _outproj(o_f, o_b, pa, shared["grp"], lc["dn_g4"], yb, yc, yd, x, mod, lc["w_out"], final_g, tm)
    return xn, s_fin


def kernel(x_prompt, x_sample, state_delta, c, c_ctx, ada_w, ada_b, norm_g, w_in, conv_qkv, a_log, dt_bias,
           dn_norm_g, sgu_norm_g, sgu_w, sgu_b, pool_w, pool_scale, fourier_w, w_out, final_norm_g):
    w = dict(norm_g=norm_g, w_in=w_in, conv_qkv=conv_qkv, a_log=a_log, dt_bias=dt_bias, dn_norm_g=dn_norm_g,
             sgu_norm_g=sgu_norm_g, sgu_w=sgu_w, sgu_b=sgu_b, pool_w=pool_w, pool_scale=pool_scale,
             fourier_w=fourier_w, w_out=w_out)
    bp, tp, _ = x_prompt.shape
    bs, ts, _ = x_sample.shape
    conds = jnp.concatenate([c, c_ctx[None], jnp.zeros((16 - bs - 1, D_MODEL), F32)], axis=0)
    mod = _modulation(conds, ada_w, ada_b)
    ones64 = jnp.ones((H_A, DK, DK), F32)
    shared = dict(sel=_gate_selector(), grp=_block_diag4(ones64).astype(BF16), band=_pool_band(),
                  tri=_chunk_triangles(DELTA_TILE))
    def dft_consts(T):
        th = min(T // 2, 512)
        return dict(cs=_dft_matrix(T, th), rev=_row_reversal(th), th=th)

    fmat_p = dft_consts(tp)
    fmat_s = dft_consts(ts)
    wz = _fold_channel_dft(w_in[:, :, 2320:2576], _channel_dft())
    layers = [_layer_consts(l, w, wz) for l in range(DEPTH)]

    xp, xs = x_prompt, x_sample
    ctx_states = []
    for l in range(DEPTH):
        fg = final_norm_g if l == DEPTH - 1 else None
        xp, s_l = _trunk_layer(xp, None, mod[l, bs:bs + 1].reshape(1, 1, 3 * D_MODEL), None, layers[l], shared,
                               fmat_p, fg)
        ctx_states.append(s_l)
        pe = _grid_pos_embed(ts) if l == 0 else None
        xs, _ = _trunk_layer(xs, pe, mod[l, 0:bs].reshape(bs, 1, 3 * D_MODEL), state_delta[:, l],
                             layers[l], shared, fmat_s, fg)
    return xp, xs, jnp.stack(ctx_states, axis=1)
```

```python
import functools
import math

import numpy as np
import jax
import jax.numpy as jnp
from jax import lax
from jax.experimental import pallas as pl
from jax.experimental.pallas import tpu as pltpu

F32 = jnp.float32
BF16 = jnp.bfloat16

D_MODEL = 1024
DEPTH = 4
GRID_W = 64
POS_BASE = 10000.0
H_A = 4
DK = 64
W_A = 256
CONV_W = 4
DN_CHUNK = 64
W_B = 256
G_B = 4
SGU_CHUNK = 128
W_C = 256
POOL_WINDOWS = (2, 4, 8, 16)
W_D = 256
G_D = 4
C_D = 64
EPS = 1e-6

P_MAIN = 3072
P_PAD = P_MAIN + 128
HALO = 16
DREC = 5 * W_A
SCAN_SEQS = 4
DELTA_TILE = 256

VMEM_LIMIT = 56 * 1024 * 1024


def _cparams(sem):
    return pltpu.CompilerParams(dimension_semantics=sem, vmem_limit_bytes=VMEM_LIMIT)


def _silu(x):
    return x * jax.nn.sigmoid(x)


def _split2(x):
    hi = x.astype(BF16)
    lo = (x - hi.astype(F32)).astype(BF16)
    return hi, lo


def _split3(x):
    hi = x.astype(BF16)
    r = x - hi.astype(F32)
    mid = r.astype(BF16)
    lo = (r - mid.astype(F32)).astype(BF16)
    return hi, mid, lo


def _dot(a, b):
    return jnp.dot(a, b, preferred_element_type=F32)


def _dot_nt(a, b):
    return lax.dot_general(a, b, (((1,), (1,)), ((), ())), preferred_element_type=F32)


def _dot_tn(a, b):
    return lax.dot_general(a, b, (((0,), (0,)), ((), ())), preferred_element_type=F32)


def _dot_exact_rhs(x, sel):
    hi, mid, lo = _split3(x)
    return _dot(hi, sel) + _dot(mid, sel) + _dot(lo, sel)


def _dot_exact_lhs(sel, x):
    hi, mid, lo = _split3(x)
    return _dot(sel, hi) + _dot(sel, mid) + _dot(sel, lo)


def _mod_kernel(c_ref, w_ref, b_ref, o_ref):
    a = _silu(c_ref[...]).astype(BF16)
    o_ref[0] = _dot(a, w_ref[0].astype(BF16)) + b_ref[0]


def _modulation(conds, ada_w, ada_b):
    nc = conds.shape[0]
    tn = 1024
    return pl.pallas_call(
        _mod_kernel,
        grid=(DEPTH, 3 * D_MODEL // tn),
        in_specs=[pl.BlockSpec((nc, D_MODEL), lambda l, n: (0, 0)),
                  pl.BlockSpec((1, D_MODEL, tn), lambda l, n: (l, 0, n)),
                  pl.BlockSpec((1, 1, tn), lambda l, n: (l, 0, n))],
        out_specs=pl.BlockSpec((1, nc, tn), lambda l, n: (l, 0, n)),
        out_shape=jax.ShapeDtypeStruct((DEPTH, nc, 3 * D_MODEL), F32),
        compiler_params=_cparams(("parallel", "parallel")),
        name="adaln_mod",
    )(conds, ada_w, ada_b.reshape(DEPTH, 1, 3 * D_MODEL))


def _inproj_body(x, mod_ref, ng_ref, w_ref, pa_ref, pb_ref, pc_ref, gd_ref, z_ref, ba_ref):
    ms = jnp.mean(x * x, axis=-1, keepdims=True)
    y = x * lax.rsqrt(ms + EPS) * ng_ref[...]
    shift = mod_ref[0, :, 0:D_MODEL]
    scale = mod_ref[0, :, D_MODEL:2 * D_MODEL]
    h = (y * (1.0 + scale) + shift).astype(BF16)
    pa_ref[0] = _dot(h, w_ref[:, 0:1024]).astype(BF16)
    pb_ref[0] = _dot(h, w_ref[:, 1024:1792]).astype(BF16)
    pc_ref[0] = _dot(h, w_ref[:, 1792:2304]).astype(BF16)
    gd_ref[0] = _dot(h, w_ref[:, 2304:2560]).astype(BF16)
    z = _dot(h, w_ref[:, 2560:3072])
    z_ref[0, 0] = z[:, 0:256].astype(BF16)
    z_ref[0, 1] = z[:, 256:512].astype(BF16)
    ba_ref[0] = _dot(h, w_ref[:, 3072:3200])


def _inproj_kernel(x_ref, mod_ref, ng_ref, w_ref, pa_ref, pb_ref, pc_ref, gd_ref, z_ref, ba_ref):
    _inproj_body(x_ref[0], mod_ref, ng_ref, w_ref, pa_ref, pb_ref, pc_ref, gd_ref, z_ref, ba_ref)


def _inproj_pos_kernel(x_ref, pe_ref, mod_ref, ng_ref, w_ref, x0_ref, pa_ref, pb_ref, pc_ref, gd_ref, z_ref, ba_ref):
    x = x_ref[0] + pe_ref[...]
    x0_ref[0] = x
    _inproj_body(x, mod_ref, ng_ref, w_ref, pa_ref, pb_ref, pc_ref, gd_ref, z_ref, ba_ref)


def _inproj(x, pe, mod, norm_g, w_in_p, tm):
    B, T, D = x.shape
    nmod = mod.shape[0]
    mod_map = (lambda b, i: (b, 0, 0)) if nmod == B else (lambda b, i: (0, 0, 0))
    tok = lambda n: pl.BlockSpec((1, tm, n), lambda b, i: (b, i, 0))
    in_specs = [pl.BlockSpec((1, 1, 3 * D), mod_map),
                pl.BlockSpec((1, D), lambda b, i: (0, 0)),
                pl.BlockSpec((D, P_PAD), lambda b, i: (0, 0))]
    out_specs = [tok(1024), tok(768), tok(512), tok(256),
                 pl.BlockSpec((1, 2, tm, 256), lambda b, i: (b, 0, i, 0)), tok(128)]
    out_shape = [jax.ShapeDtypeStruct((B, T, 1024), BF16),
                 jax.ShapeDtypeStruct((B, T, 768), BF16),
                 jax.ShapeDtypeStruct((B, T, 512), BF16),
                 jax.ShapeDtypeStruct((B, T, 256), BF16),
                 jax.ShapeDtypeStruct((B, 2, T, 256), BF16),
                 jax.ShapeDtypeStruct((B, T, 128), F32)]
    args = [mod, norm_g.reshape(1, D), w_in_p]
    if pe is None:
        kern, in_specs, args = _inproj_kernel, [tok(D)] + in_specs, [x] + args
    else:
        kern = _inproj_pos_kernel
        in_specs = [tok(D), pl.BlockSpec((tm, D), lambda b, i: (i, 0))] + in_specs
        args = [x, pe] + args
        out_specs = [tok(D)] + out_specs
        out_shape = [jax.ShapeDtypeStruct((B, T, D), F32)] + out_shape
    return pl.pallas_call(
        kern,
        grid=(B, T // tm),
        in_specs=in_specs,
        out_specs=out_specs,
        out_shape=out_shape,
        compiler_params=_cparams(("parallel", "parallel")),
        name="in_proj",
    )(*args)


def _fold_kernel(w_ref, cs_ref, o_ref):
    o_ref[0] = _dot_exact_rhs(w_ref[0], cs_ref[...])


def _fold_channel_dft(w_xd, cs):
    return pl.pallas_call(
        _fold_kernel,
        grid=(DEPTH,),
        in_specs=[pl.BlockSpec((1, D_MODEL, 256), lambda l: (l, 0, 0)),
                  pl.BlockSpec((256, 512), lambda l: (0, 0))],
        out_specs=pl.BlockSpec((1, D_MODEL, 512), lambda l: (l, 0, 0)),
        out_shape=jax.ShapeDtypeStruct((DEPTH, D_MODEL, 512), F32),
        compiler_params=_cparams(("parallel",)),
        name="fold_channel_dft",
    )(w_xd, cs)


def _outproj_body(of_ref, ob_ref, ga_ref, grp_ref, ng_ref, yb_ref, yc_ref, yd_ref, x_ref, mod_ref, w_ref):
    o = of_ref[0] + ob_ref[0]
    ms = _dot((o * o).astype(BF16), grp_ref[...]) * (1.0 / DK)
    ya = (o * lax.rsqrt(ms + EPS) * ng_ref[...] * _silu(ga_ref[0].astype(F32))).astype(BF16)
    ycat = jnp.concatenate([ya, yb_ref[0], yc_ref[0], yd_ref[0]], axis=-1)
    out = _dot(ycat, w_ref[...])
    gate = mod_ref[0, :, 2 * D_MODEL:3 * D_MODEL]
    return x_ref[0] + gate * out


def _outproj_final_kernel(of_ref, ob_ref, ga_ref, grp_ref, ng_ref, yb_ref, yc_ref, yd_ref, x_ref, mod_ref, w_ref,
                          fg_ref, o_ref):
    xn = _outproj_body(of_ref, ob_ref, ga_ref, grp_ref, ng_ref, yb_ref, yc_ref, yd_ref, x_ref, mod_ref, w_ref)
    ms = jnp.mean(xn * xn, axis=-1, keepdims=True)
    o_ref[0] = xn * lax.rsqrt(ms + EPS) * fg_ref[...]


def _outproj_inproj_kernel(of_ref, ob_ref, ga_ref, grp_ref, ng_ref, yb_ref, yc_ref, yd_ref, x_ref, mod_ref, w_ref,
                           modn_ref, ngn_ref, win_ref, xn_ref, pa_ref, pb_ref, pc_ref, gd_ref, z_ref, ba_ref):
    xn = _outproj_body(of_ref, ob_ref, ga_ref, grp_ref, ng_ref, yb_ref, yc_ref, yd_ref, x_ref, mod_ref, w_ref)
    xn_ref[0] = xn
    _inproj_body(xn, modn_ref, ngn_ref, win_ref, pa_ref, pb_ref, pc_ref, gd_ref, z_ref, ba_ref)


def _outproj_specs(x, mod, tm):
    B, T, D = x.shape
    mod_map = (lambda b, i: (b, 0, 0)) if mod.shape[0] == B else (lambda b, i: (0, 0, 0))
    tok = lambda n: pl.BlockSpec((1, tm, n), lambda b, i: (b, i, 0))
    in_specs = [tok(256), tok(256),
                pl.BlockSpec((1, tm, 256), lambda b, i: (b, i, 3)),
                pl.BlockSpec((256, 256), lambda b, i: (0, 0)),
                pl.BlockSpec((1, 256), lambda b, i: (0, 0)),
                tok(256), tok(256), tok(256), tok(D),
                pl.BlockSpec((1, 1, 3 * D), mod_map),
                pl.BlockSpec((D, D), lambda b, i: (0, 0))]
    return in_specs, tok, mod_map


def _outproj_final(o_f, o_b, pa, grp, ng4, yb, yc, yd, x, mod, w_out, final_g, tm):
    B, T, D = x.shape
    in_specs, tok, _ = _outproj_specs(x, mod, tm)
    return pl.pallas_call(
        _outproj_final_kernel,
        grid=(B, T // tm),
        in_specs=in_specs + [pl.BlockSpec((1, D), lambda b, i: (0, 0))],
        out_specs=tok(D),
        out_shape=jax.ShapeDtypeStruct((B, T, D), F32),
        compiler_params=_cparams(("parallel", "parallel")),
        name="out_proj",
    )(o_f, o_b, pa, grp, ng4, yb, yc, yd, x, mod, w_out, final_g.reshape(1, D))


def _outproj_inproj(o_f, o_b, pa, grp, ng4, yb, yc, yd, x, mod, w_out, norm_g_next, w_in_next, tm):
    B, T, D = x.shape
    mod_cur, mod_next = mod
    in_specs, tok, mod_map = _outproj_specs(x, mod_cur, tm)
    in_specs += [pl.BlockSpec((1, 1, 3 * D), mod_map),
                 pl.BlockSpec((1, D), lambda b, i: (0, 0)),
                 pl.BlockSpec((D, P_PAD), lambda b, i: (0, 0))]
    out_specs = [tok(D), tok(1024), tok(768), tok(512), tok(256),
                 pl.BlockSpec((1, 2, tm, 256), lambda b, i: (b, 0, i, 0)), tok(128)]
    out_shape = [jax.ShapeDtypeStruct((B, T, D), F32),
                 jax.ShapeDtypeStruct((B, T, 1024), BF16),
                 jax.ShapeDtypeStruct((B, T, 768), BF16),
                 jax.ShapeDtypeStruct((B, T, 512), BF16),
                 jax.ShapeDtypeStruct((B, T, 256), BF16),
                 jax.ShapeDtypeStruct((B, 2, T, 256), BF16),
                 jax.ShapeDtypeStruct((B, T, 128), F32)]
    return pl.pallas_call(
        _outproj_inproj_kernel,
        grid=(B, T // tm),
        in_specs=in_specs,
        out_specs=out_specs,
        out_shape=out_shape,
        compiler_params=_cparams(("parallel", "parallel")),
        name="out_in_proj",
    )(o_f, o_b, pa, grp, ng4, yb, yc, yd, x, mod_cur, w_out, mod_next, norm_g_next.reshape(1, D), w_in_next)


def _gelu_tanh(x):
    return 0.5 * x * (1.0 + jnp.tanh(math.sqrt(2.0 / math.pi) * (x + 0.044715 * (x * x * x))))


def _diag_blocks(r, lane_grp):
    n = r.shape[0] // 4
    out = r[0:n]
    for g in range(1, 4):
        out = jnp.where(lane_grp == g, r[g * n:(g + 1) * n], out)
    return out


def _local_kernel(pb_ref, pc_ref, prev_ref, next_ref, sgw_ref, sgb_ref, sgn_ref, band_ref, pw_ref, ps_ref,
                  yb_ref, yc_ref, *, T, tl):
    i = pl.program_id(1)
    nblk = T // tl
    n = SGU_CHUNK
    lane_grp = lax.broadcasted_iota(jnp.int32, (n, 256), 1) // 64
    row = lax.broadcasted_iota(jnp.int32, (n, 256), 0)
    lo_w = jnp.zeros((n, 256), jnp.int32)
    hi_w = jnp.zeros((n, 256), jnp.int32)
    for g, w in enumerate(POOL_WINDOWS):
        lo_w = jnp.where(lane_grp == g, w // 2, lo_w)
        hi_w = jnp.where(lane_grp == g, w - w // 2, hi_w)
    prev_blk = jnp.where(i > 0, prev_ref[0, :, 0:256], jnp.zeros((HALO, 256), BF16))
    next_blk = jnp.where(i < nblk - 1, next_ref[0, :, 0:256], jnp.zeros((HALO, 256), BF16))
    nch = tl // n
    for j in range(nch):
        r0 = j * n
        uvg = pb_ref[0, r0:r0 + n, :]
        gel = _gelu_tanh(uvg[:, 0:512].astype(F32))
        u = gel[:, 0:256]
        v = gel[:, 256:512]
        vn = v * lax.rsqrt(jnp.mean(v * v, axis=-1, keepdims=True) + EPS) * sgn_ref[...]
        sv = _diag_blocks(_dot(sgw_ref[...], vn.astype(BF16)), lane_grp) + sgb_ref[...]
        yb_ref[0, r0:r0 + n, :] = (u * sv * _silu(uvg[:, 512:768].astype(F32))).astype(BF16)
        xt = pc_ref[0, r0:r0 + n, 0:256]
        pv = prev_blk if j == 0 else pc_ref[0, r0 - HALO:r0, 0:256]
        nx = next_blk if j == nch - 1 else pc_ref[0, r0 + n:r0 + n + HALO, 0:256]
        xcat = jnp.concatenate([pv, xt, nx], axis=0)
        wsum = _diag_blocks(_dot(band_ref[...], xcat), lane_grp)
        t = i * tl + r0 + row
        cnt = (jnp.minimum(t + hi_w, T) - jnp.maximum(t - lo_w, 0)).astype(F32)
        pooled = wsum / cnt - xt.astype(F32)
        yc = _dot(pooled.astype(BF16), pw_ref[...]) * ps_ref[...]
        yc_ref[0, r0:r0 + n, :] = (yc * _silu(pc_ref[0, r0:r0 + n, 256:512].astype(F32))).astype(BF16)


def _local(pb, pc, sgw, sgb, sgn, band, pw_bd, pscale, tl):
    B, T, _ = pb.shape
    hb = tl // HALO
    nh = T // HALO
    const = lambda shape: pl.BlockSpec(shape, lambda b, i: (0,) * len(shape))
    return pl.pallas_call(
        functools.partial(_local_kernel, T=T, tl=tl),
        grid=(B, T // tl),
        in_specs=[pl.BlockSpec((1, tl, 768), lambda b, i: (b, i, 0)),
                  pl.BlockSpec((1, tl, 512), lambda b, i: (b, i, 0)),
                  pl.BlockSpec((1, HALO, 512), lambda b, i: (b, jnp.maximum(i * hb - 1, 0), 0)),
                  pl.BlockSpec((1, HALO, 512), lambda b, i: (b, jnp.minimum((i + 1) * hb, nh - 1), 0)),
                  const((4 * SGU_CHUNK, SGU_CHUNK)), const((SGU_CHUNK, 256)), const((1, 256)),
                  const((4 * SGU_CHUNK, SGU_CHUNK + 2 * HALO)), const((256, 256)), const((1, 256))],
        out_specs=[pl.BlockSpec((1, tl, 256), lambda b, i: (b, i, 0)),
                   pl.BlockSpec((1, tl, 256), lambda b, i: (b, i, 0))],
        out_shape=[jax.ShapeDtypeStruct((B, T, 256), BF16), jax.ShapeDtypeStruct((B, T, 256), BF16)],
        compiler_params=_cparams(("parallel", "parallel")),
        name="sgu_pool",
    )(pb, pc, pc, pc, sgw, sgb, sgn, band, pw_bd, pscale)


DFT_EXTRA = 8


def _dft_kernel(f_ref, z_ref, glo_ref, ghi_ref, rev_ref, fw_ref, ylo_ref, yhi_ref, *, T, th, inv_norm):
    p = _dot(f_ref[:, 0:T], z_ref[0, 0:T, :])
    q = _dot(f_ref[:, T:2 * T], z_ref[0, T:2 * T, :])
    f_lo = ((p + q) * inv_norm)[0:th].astype(BF16)
    f_hi = ((p - q) * inv_norm)[1:th + 1].astype(BF16)
    f_hi = _dot(rev_ref[...], f_hi).astype(BF16)
    ylo_ref[0] = (_dot(f_lo, fw_ref[...]) * _silu(glo_ref[0].astype(F32))).astype(BF16)
    yhi_ref[0] = (_dot(f_hi, fw_ref[...]) * _silu(ghi_ref[0].astype(F32))).astype(BF16)


def _fourier(z, gd, fmat, rev, fw_bd, th):
    B, T, _ = gd.shape
    nh = T // 2 // th
    z = z.reshape(B, 2 * T, 256)
    y_lo, y_hi = pl.pallas_call(
        functools.partial(_dft_kernel, T=T, th=th, inv_norm=1.0 / math.sqrt(T * C_D)),
        grid=(nh, B),
        in_specs=[pl.BlockSpec((th + DFT_EXTRA, 2 * T), lambda i, b: (i, 0)),
                  pl.BlockSpec((1, 2 * T, 256), lambda i, b: (b, 0, 0)),
                  pl.BlockSpec((1, th, 256), lambda i, b: (b, i, 0)),
                  pl.BlockSpec((1, th, 256), lambda i, b: (b, 2 * nh - 1 - i, 0)),
                  pl.BlockSpec((th, th), lambda i, b: (0, 0)),
                  pl.BlockSpec((256, 256), lambda i, b: (0, 0))],
        out_specs=[pl.BlockSpec((1, th, 256), lambda i, b: (b, i, 0)),
                   pl.BlockSpec((1, th, 256), lambda i, b: (b, nh - 1 - i, 0))],
        out_shape=[jax.ShapeDtypeStruct((B, T // 2, 256), BF16), jax.ShapeDtypeStruct((B, T // 2, 256), BF16)],
        compiler_params=_cparams(("parallel", "parallel")),
        name="dft_positions",
    )(fmat, z, gd, gd, rev, fw_bd)
    return jnp.concatenate([y_lo, y_hi], axis=1)


TRI_UNITS = 128
TRI_STRIDE = DN_CHUNK + 8


def _expand(x, bd):
    return jnp.concatenate([x, x, x, x], axis=0) * bd


def _delta_prep_kernel(qkv_ref, prev_ref, next_ref, ba_ref, cw_ref, par_ref, sel_ref, grp_ref, tri_ref,
                       d0_ref, d1_ref, a_ref, gl_ref, *, T, tc):
    j = pl.program_id(1)
    nblk = T // tc
    x = qkv_ref[0].astype(F32)
    prev = jnp.where(j > 0, prev_ref[0].astype(F32), 0.0)
    nxt = jnp.where(j < nblk - 1, next_ref[0].astype(F32), 0.0)
    xcat = jnp.concatenate([prev, x, nxt], axis=0)
    n = tc + 2 * HALO
    acc = None
    for tap in range(CONV_W):
        s = (CONV_W // 2 - tap) % n
        xs = xcat if s == 0 else pltpu.roll(xcat, s, axis=0)
        term = xs[HALO:HALO + tc] * cw_ref[tap:tap + 1, :]
        acc = term if acc is None else acc + term
    c = _silu(acc)
    cq, ck, v = c[:, 0:256], c[:, 256:512], c[:, 512:768]
    grp = grp_ref[...]
    q = cq * lax.rsqrt(_dot((cq * cq).astype(BF16), grp) + EPS) * (DK ** -0.5)
    k = ck * lax.rsqrt(_dot((ck * ck).astype(BF16), grp) + EPS)
    ba = ba_ref[0].T[0:4 * H_A]
    z = ba + par_ref[1]
    softplus = jnp.maximum(z, 0.0) + jnp.log1p(jnp.exp(-jnp.abs(z)))
    row = lax.broadcasted_iota(jnp.int32, ba.shape, 0)
    bg = jnp.where(row < 2 * H_A, jax.nn.sigmoid(ba), -jnp.exp(par_ref[0]) * softplus)
    gsum_f = _dot_exact_rhs(bg, tri_ref[1])
    gsum_b = _dot_exact_rhs(bg, tri_ref[0])
    narrow = jnp.where(row < 2 * H_A, bg, jnp.where(row < 3 * H_A, gsum_f, gsum_b))
    hi, mid, lo = _split3(narrow)
    sel = sel_ref[...]
    bgx = _dot_tn(hi, sel) + _dot_tn(mid, sel) + _dot_tn(lo, sel)
    gcum = (bgx[:, 512:768], bgx[:, 768:1024])

    bd = grp
    r64 = lax.broadcasted_iota(jnp.int32, (DN_CHUNK, 256), 0)
    c64 = lax.broadcasted_iota(jnp.int32, (DN_CHUNK, 256), 1) % 64
    diag64 = r64 == c64
    pad_rows = jnp.zeros((TRI_STRIDE - DN_CHUNK, 128), F32)
    out_refs = (d0_ref, d1_ref)
    for ch in range(tc // DN_CHUNK):
        r0 = ch * DN_CHUNK
        a0 = ch * TRI_STRIDE
        kx = k[r0:r0 + DN_CHUNK]
        qx = q[r0:r0 + DN_CHUNK]
        vx = v[r0:r0 + DN_CHUNK]
        kb = kx.astype(BF16)
        kq = _dot_nt(jnp.concatenate([kb, qx.astype(BF16)], axis=0), _expand(kb, bd))
        kk, qk = kq[0:DN_CHUNK], kq[DN_CHUNK:2 * DN_CHUNK]
        for d in range(2):
            gc = gcum[d][r0:r0 + DN_CHUNK]
            beta = bgx[r0:r0 + DN_CHUNK, d * 256:(d + 1) * 256]
            rowv = jnp.sum(jnp.where(diag64, gc, 0.0), axis=0, keepdims=True)
            incl = (r64 >= c64) if d == 0 else (r64 <= c64)
            strict = (r64 > c64) if d == 0 else (r64 < c64)
            decay = jnp.where(incl, jnp.exp(jnp.where(incl, gc - rowv, 0.0)), 0.0)
            a_c = jnp.where(strict, beta * kk * decay, 0.0)
            for half in range(2):
                a_ref[d, half, a0:a0 + DN_CHUNK, :] = a_c[:, 128 * half:128 * half + 128]
                a_ref[d, half, a0 + DN_CHUNK:a0 + TRI_STRIDE, :] = pad_rows
            egc = jnp.exp(gc)
            g_last = gc[DN_CHUNK - 1:DN_CHUNK] if d == 0 else gc[0:1]
            o = out_refs[d]
            o[0, r0:r0 + DN_CHUNK, 0:256] = (vx * beta).astype(BF16)
            o[0, r0:r0 + DN_CHUNK, 256:512] = (kx * beta * egc).astype(BF16)
            o[0, r0:r0 + DN_CHUNK, 512:768] = (qx * egc).astype(BF16)
            o[0, r0:r0 + DN_CHUNK, 768:1024] = (kx * jnp.exp(g_last - gc)).astype(BF16)
            o[0, r0:r0 + DN_CHUNK, 1024:1280] = (qk * decay).astype(BF16)
            gl_ref[0, d, ch] = jnp.exp(g_last)


def _delta_prep(pa, ba, conv_w, par, sel, grp, tri, tc):
    B, T, _ = pa.shape
    hb = tc // HALO
    nh = T // HALO
    nt = T // tc
    nch = tc // DN_CHUNK
    const = lambda shape: pl.BlockSpec(shape, lambda b, i: (0,) * len(shape))
    return pl.pallas_call(
        functools.partial(_delta_prep_kernel, T=T, tc=tc),
        grid=(B, nt),
        in_specs=[pl.BlockSpec((1, tc, 768), lambda b, i: (b, i, 0)),
                  pl.BlockSpec((1, HALO, 768), lambda b, i: (b, jnp.maximum(i * hb - 1, 0), 0)),
                  pl.BlockSpec((1, HALO, 768), lambda b, i: (b, jnp.minimum((i + 1) * hb, nh - 1), 0)),
                  pl.BlockSpec((1, tc, 128), lambda b, i: (b, i, 0)),
                  const((CONV_W, 768)), const((2, 4 * H_A, tc)), const((4 * H_A, 1024)), const((256, 256)),
                  const((2, tc, tc))],
        out_specs=[pl.BlockSpec((1, tc, DREC), lambda b, i: (b, i, 0)),
                   pl.BlockSpec((1, tc, DREC), lambda b, i: (b, i, 0)),
                   pl.BlockSpec((2, 2, nch * TRI_STRIDE, 128), lambda b, i: (0, 0, b * nt + i, 0)),
                   pl.BlockSpec((1, 2, nch, 1, 256), lambda b, i: (b, 0, i, 0, 0))],
        out_shape=[jax.ShapeDtypeStruct((B, T, DREC), BF16),
                   jax.ShapeDtypeStruct((B, T, DREC), BF16),
                   jax.ShapeDtypeStruct((2, 2, B * nt * nch * TRI_STRIDE, 128), F32),
                   jax.ShapeDtypeStruct((B, 2, T // DN_CHUNK, 1, 256), F32)],
        compiler_params=_cparams(("parallel", "parallel")),
        name="delta_prep",
    )(pa, pa, pa, ba, conv_w, par, sel, grp, tri)


TRI_ROWS = TRI_UNITS * TRI_STRIDE


def _tri_rows(x_ref, lower):
    sub = lax.broadcasted_iota(jnp.int32, (8, TRI_UNITS), 0)
    nb = DN_CHUNK // 8
    for ib in (range(nb) if lower else reversed(range(nb))):
        segs = range(ib + 1) if lower else range(ib, nb)
        live = range(ib + 1) if lower else range(ib, nb)

        def row(step, carry, ib=ib, segs=segs, live=live):
            i = 8 * ib + (step if lower else 7 - step)
            for heads in ((0, 1), (2, 3)):
                acc = {(h, cv): jnp.zeros((8, TRI_UNITS), F32) for h in heads for cv in live}
                for jb in segs:
                    cvs = range(jb + 1) if lower else range(jb, nb)
                    for j in range(8 * jb, 8 * jb + 8):
                        for h in heads:
                            coef = jnp.broadcast_to(x_ref[i, pl.ds(h * DK + j, 1), :], (8, TRI_UNITS))
                            for cv in cvs:
                                acc[h, cv] = acc[h, cv] + coef * x_ref[j, h * DK + 8 * cv:h * DK + 8 * cv + 8, :]
                for h in heads:
                    for cv in range(nb):
                        val = (jnp.where(sub + 8 * cv == i, 1.0, -acc[h, cv]) if cv in live
                               else jnp.zeros((8, TRI_UNITS), F32))
                        x_ref[i, h * DK + 8 * cv:h * DK + 8 * cv + 8, :] = val
            return carry

        lax.fori_loop(0, 8, row, 0)


def _tri_inverse_kernel(a_ref, t_ref, x_ref):
    d = pl.program_id(0)

    def load(i, carry):
        for half in range(2):
            x_ref[i, 128 * half:128 * half + 128, :] = a_ref[0, half, pl.ds(i, TRI_UNITS, stride=TRI_STRIDE), :].T
        return carry

    lax.fori_loop(0, DN_CHUNK, load, 0, unroll=4)

    @pl.when(d == 0)
    def _():
        _tri_rows(x_ref, True)

    @pl.when(d == 1)
    def _():
        _tri_rows(x_ref, False)

    def store(i, carry):
        for half in range(2):
            t_ref[0, half, pl.ds(i, TRI_UNITS, stride=TRI_STRIDE), :] = x_ref[i, 128 * half:128 * half + 128, :].T
        return carry

    lax.fori_loop(0, DN_CHUNK, store, 0, unroll=4)
    for half in range(2):
        for p in range(DN_CHUNK, TRI_STRIDE):
            t_ref[0, half, pl.ds(p, TRI_UNITS, stride=TRI_STRIDE), :] = jnp.zeros((TRI_UNITS, 128), F32)


def _tri_inverse(a):
    rows = a.shape[2]
    spec = pl.BlockSpec((1, 2, TRI_ROWS, 128), lambda d, i: (d, 0, i, 0))
    return pl.pallas_call(
        _tri_inverse_kernel,
        grid=(2, rows // TRI_ROWS),
        in_specs=[spec],
        out_specs=spec,
        out_shape=jax.ShapeDtypeStruct(a.shape, F32),
        scratch_shapes=[pltpu.VMEM((DN_CHUNK, 256, TRI_UNITS), F32)],
        compiler_params=_cparams(("parallel", "parallel")),
        name="tri_inverse",
    )(a)


def _delta_scan_kernel_with_state(d0_ref, d1_ref, tf_ref, tb_ref, glf_ref, glb_ref, grp_ref, s0_ref, of_ref, ob_ref,
                                  sfin_ref, s_ref, *, tc, nb):
    _delta_scan_kernel(s0_ref, d0_ref, d1_ref, tf_ref, tb_ref, glf_ref, glb_ref, grp_ref, of_ref, ob_ref, sfin_ref,
                       s_ref, tc=tc, nb=nb)


def _delta_scan_kernel(s0_ref, d0_ref, d1_ref, tf_ref, tb_ref, glf_ref, glb_ref, grp_ref, of_ref, ob_ref, sfin_ref,
                       s_ref, *, tc, nb):
    j = pl.program_id(1)
    nblk = pl.num_programs(1)
    nch = tc // DN_CHUNK

    @pl.when(j == 0)
    def _():
        if s0_ref is None:
            s_ref[...] = jnp.zeros(s_ref.shape, F32)
        else:
            zero = jnp.zeros((DK, DK), F32)
            for q in range(nb):
                for d in range(2):
                    s_ref[q, d] = jnp.concatenate(
                        [jnp.concatenate([s0_ref[q, d, h] if g == h else zero for g in range(H_A)], axis=1)
                         for h in range(H_A)], axis=0)

    bd = grp_ref[...]
    bd_f32 = bd.astype(F32)
    rec = (d0_ref, d1_ref)
    tri = (tf_ref, tb_ref)
    glr = (glf_ref, glb_ref)
    out = (of_ref, ob_ref)
    chains = [(q, d) for q in range(nb) for d in range(2)]

    def chunk_of(d, step):
        return step if d == 0 else nch - 1 - step

    def rows_of(c):
        return slice(c * DN_CHUNK, (c + 1) * DN_CHUNK)

    uw = {}
    for step in range(nch):
        for q, d in chains:
            c = chunk_of(d, step)
            trows = slice(c * TRI_STRIDE, c * TRI_STRIDE + DN_CHUNK)
            tinv = jnp.concatenate([tri[d][0, 0, q, trows, :], tri[d][0, 1, q, trows, :]], axis=1).astype(BF16)
            rhs = jnp.concatenate([_expand(rec[d][q, rows_of(c), 0:256], bd),
                                   _expand(rec[d][q, rows_of(c), 256:512], bd)], axis=1)
            uw[q, d, step] = _dot(tinv, rhs)

    for step in range(nch):
        s = {k: s_ref[k[0], k[1]] for k in chains}
        ws_qs = {}
        for q, d in chains:
            r = rows_of(chunk_of(d, step))
            wq = jnp.concatenate([uw[q, d, step][:, 256:512].astype(BF16), rec[d][q, r, 512:768]], axis=0)
            ws_qs[q, d] = _dot(wq, s[q, d].astype(BF16))
        v_new = {k: (uw[k[0], k[1], step][:, 0:256] - ws_qs[k][0:DN_CHUNK]).astype(BF16) for k in chains}
        for q, d in chains:
            c = chunk_of(d, step)
            r = rows_of(c)
            out[d][q, r, :] = ws_qs[q, d][DN_CHUNK:2 * DN_CHUNK] + _dot(rec[d][q, r, 1024:1280], _expand(v_new[q, d], bd))
            upd = _dot_tn(rec[d][q, r, 768:1024], v_new[q, d])
            s_ref[q, d] = s[q, d] * glr[d][q, 0, c] + upd * bd_f32

    @pl.when(j == nblk - 1)
    def _():
        for q in range(nb):
            for d in range(2):
                s = s_ref[q, d]
                for h in range(H_A):
                    sfin_ref[q, d, h] = s[h * DK:(h + 1) * DK, h * DK:(h + 1) * DK]


def _delta_scan(d0, d1, tinv, gl, grp, s0, tc, nb):
    B, T, _ = d0.shape
    nblk = T // tc
    nch = tc // DN_CHUNK
    tinv = tinv.reshape(2, 2, B, nblk * nch * TRI_STRIDE, 128)
    state_spec = pl.BlockSpec((nb, 2, H_A, DK, DK), lambda b, j: (b, 0, 0, 0, 0))
    in_specs = [pl.BlockSpec((nb, tc, DREC), lambda b, j: (b, j, 0)),
                pl.BlockSpec((nb, tc, DREC), lambda b, j: (b, nblk - 1 - j, 0)),
                pl.BlockSpec((1, 2, nb, nch * TRI_STRIDE, 128), lambda b, j: (0, 0, b, j, 0)),
                pl.BlockSpec((1, 2, nb, nch * TRI_STRIDE, 128), lambda b, j: (1, 0, b, nblk - 1 - j, 0)),
                pl.BlockSpec((nb, 1, nch, 1, 256), lambda b, j: (b, 0, j, 0, 0)),
                pl.BlockSpec((nb, 1, nch, 1, 256), lambda b, j: (b, 1, nblk - 1 - j, 0, 0)),
                pl.BlockSpec((256, 256), lambda b, j: (0, 0))]
    args = [d0, d1, tinv, tinv, gl, gl, grp]
    if s0 is None:
        kern = functools.partial(_delta_scan_kernel, None, tc=tc, nb=nb)
    else:
        kern = functools.partial(_delta_scan_kernel_with_state, tc=tc, nb=nb)
        in_specs.append(state_spec)
        args.append(s0)
    return pl.pallas_call(
        kern,
        grid=(B // nb, nblk),
        in_specs=in_specs,
        out_specs=[pl.BlockSpec((nb, tc, 256), lambda b, j: (b, j, 0)),
                   pl.BlockSpec((nb, tc, 256), lambda b, j: (b, nblk - 1 - j, 0)),
                   state_spec],
        out_shape=[jax.ShapeDtypeStruct((B, T, 256), F32),
                   jax.ShapeDtypeStruct((B, T, 256), F32),
                   jax.ShapeDtypeStruct((B, 2, H_A, DK, DK), F32)],
        scratch_shapes=[pltpu.VMEM((nb, 2, 256, 256), F32)],
        compiler_params=_cparams(("parallel", "arbitrary")),
        name="delta_scan",
    )(*args)


def _block_diag4(w):
    n = w.shape[-1]
    eye = jnp.eye(4, dtype=w.dtype)
    return jnp.einsum("gij,gh->gihj", w, eye).reshape(4 * n, 4 * n)


def _dft_matrix(T, th):
    w = 2.0 * math.pi / T
    nh = T // 2 // th
    groups = (th + DFT_EXTRA) // 8
    t = jnp.arange(T, dtype=jnp.int32)[None, None, :]
    coarse = (th * jnp.arange(nh, dtype=jnp.int32)[:, None] + 8 * jnp.arange(groups, dtype=jnp.int32)[None, :])
    coarse = coarse.reshape(nh * groups, 1, 1)
    fine = jnp.arange(8, dtype=jnp.int32)[None, :, None]
    ang_a = ((coarse * t) % T).astype(F32) * w
    ang_b = ((fine * t) % T).astype(F32) * w
    ca, sa, cb, sb = jnp.cos(ang_a), jnp.sin(ang_a), jnp.cos(ang_b), jnp.sin(ang_b)
    rows = nh * (th + DFT_EXTRA)
    cos_m = (ca * cb - sa * sb).reshape(rows, T)
    sin_m = (sa * cb + ca * sb).reshape(rows, T)
    return jnp.concatenate([cos_m, sin_m], axis=1).astype(BF16)


def _row_reversal(n):
    r = jnp.arange(n)
    return (r[:, None] + r[None, :] == n - 1).astype(BF16)


def _channel_dft():
    c = jnp.arange(C_D, dtype=jnp.int32)
    ang = ((c[:, None] * c[None, :]) % C_D).astype(F32) * (2.0 * math.pi / C_D)
    cos4 = _block_diag4(jnp.broadcast_to(jnp.cos(ang), (G_D, C_D, C_D)))
    sin4 = _block_diag4(jnp.broadcast_to(jnp.sin(ang), (G_D, C_D, C_D)))
    return jnp.concatenate([cos4, -sin4], axis=1).astype(BF16)


def _pool_band():
    n = SGU_CHUNK
    t = jnp.arange(n)[:, None]
    s = jnp.arange(n + 2 * HALO)[None, :] - HALO
    blocks = [((s >= t - w // 2) & (s < t + w - w // 2)) for w in POOL_WINDOWS]
    return jnp.concatenate(blocks, axis=0).astype(BF16)


def _chunk_triangles(tc):
    r = jnp.arange(tc)[:, None]
    c = jnp.arange(tc)[None, :]
    same = (r // DN_CHUNK) == (c // DN_CHUNK)
    return jnp.stack([same & (r >= c), same & (r <= c)]).astype(BF16)


def _gate_selector():
    src = jnp.arange(4 * H_A)[:, None]
    dst = jnp.arange(1024)[None, :]
    kind, d, h = dst // 512, (dst // 256) % 2, (dst // 64) % 4
    return (src == kind * 8 + d * 4 + h).astype(BF16)


def _grid_pos_embed(T):
    rows = T // GRID_W
    r = jnp.repeat(jnp.arange(rows, dtype=F32), GRID_W)
    col = jnp.tile(jnp.arange(GRID_W, dtype=F32), rows)
    n_freq = D_MODEL // 4
    freqs = jnp.power(POS_BASE, -jnp.arange(n_freq, dtype=F32) / n_freq)
    ar = r[:, None] * freqs[None]
    ac = col[:, None] * freqs[None]
    return jnp.concatenate([jnp.sin(ar), jnp.cos(ar), jnp.sin(ac), jnp.cos(ac)], axis=-1)


def _layer_consts(l, w, wz):
    win = w["w_in"][l]
    w_in_p = jnp.concatenate([win[:, 0:1024], win[:, 1040:2320], win[:, 2576:2832], wz[l], win[:, 1024:1040],
                              jnp.zeros((D_MODEL, 128 - 4 * H_A), F32)], axis=1).astype(BF16)
    par = jnp.stack([jnp.concatenate([jnp.zeros((2 * H_A,), F32), w[name][l].reshape(2 * H_A)])
                     for name in ("a_log", "dt_bias")])
    par = jnp.broadcast_to(par[:, :, None], (2, 4 * H_A, DELTA_TILE))
    return dict(
        norm_g=w["norm_g"][l], w_in_p=w_in_p, conv_w=w["conv_qkv"][l], par=par,
        dn_g4=jnp.tile(w["dn_norm_g"][l], H_A).reshape(1, 256),
        sgw=w["sgu_w"][l].reshape(G_B * SGU_CHUNK, SGU_CHUNK).astype(BF16),
        sgb=jnp.repeat(w["sgu_b"][l].T, W_B // G_B, axis=1),
        sgn=w["sgu_norm_g"][l].reshape(1, W_B),
        pw_bd=_block_diag4(w["pool_w"][l]).astype(BF16),
        pscale=w["pool_scale"][l].reshape(1, W_C),
        fw_bd=_block_diag4(w["fourier_w"][l]).astype(BF16),
        w_out=w["w_out"][l].astype(BF16),
    )


def _trunk(x, pe, mods, states, layers, shared, fmat, final_g):
    B, T, _ = x.shape
    tm = min(T, 512)
    proj = _inproj(x, pe, mods[0], layers[0]["norm_g"], layers[0]["w_in_p"], tm)
    if pe is not None:
        x, proj = proj[0], proj[1:]
    finals = []
    for l, lc in enumerate(layers):
        pa, pb, pc, gd, z, ba = proj
        d0, d1, a_tri, gl = _delta_prep(pa, ba, lc["conv_w"], lc["par"], shared["sel"], shared["grp"], shared["tri"],
                                        DELTA_TILE)
        s0 = None if states is None else states[l]
        o_f, o_b, s_fin = _delta_scan(d0, d1, _tri_inverse(a_tri), gl, shared["grp"], s0, DELTA_TILE, SCAN_SEQS)
        finals.append(s_fin)
        yb, yc = _local(pb, pc, lc["sgw"], lc["sgb"], lc["sgn"], shared["band"], lc["pw_bd"], lc["pscale"], tm)
        yd = _fourier(z, gd, fmat["cs"], fmat["rev"], lc["fw_bd"], fmat["th"])
        mix = (o_f, o_b, pa, shared["grp"], lc["dn_g4"], yb, yc, yd, x)
        if l + 1 < len(layers):
            nxt = layers[l + 1]
            x, *proj = _outproj_inproj(*mix, (mods[l], mods[l + 1]), lc["w_out"], nxt["norm_g"], nxt["w_in_p"], tm)
        else:
            x = _outproj_final(*mix, mods[l], lc["w_out"], final_g, tm)
    return x, finals


def kernel(x_prompt, x_sample, state_delta, c, c_ctx, ada_w, ada_b, norm_g, w_in, conv_qkv, a_log, dt_bias,
           dn_norm_g, sgu_norm_g, sgu_w, sgu_b, pool_w, pool_scale, fourier_w, w_out, final_norm_g):
    w = dict(norm_g=norm_g, w_in=w_in, conv_qkv=conv_qkv, a_log=a_log, dt_bias=dt_bias, dn_norm_g=dn_norm_g,
             sgu_norm_g=sgu_norm_g, sgu_w=sgu_w, sgu_b=sgu_b, pool_w=pool_w, pool_scale=pool_scale,
             fourier_w=fourier_w, w_out=w_out)
    bp, tp, _ = x_prompt.shape
    bs, ts, _ = x_sample.shape
    conds = jnp.concatenate([c, c_ctx[None], jnp.zeros((16 - bs - 1, D_MODEL), F32)], axis=0)
    mod = _modulation(conds, ada_w, ada_b)
    ones64 = jnp.ones((H_A, DK, DK), F32)
    shared = dict(sel=_gate_selector(), grp=_block_diag4(ones64).astype(BF16), band=_pool_band(),
                  tri=_chunk_triangles(DELTA_TILE))
    def dft_consts(T):
        th = min(T // 2, 512)
        return dict(cs=_dft_matrix(T, th), rev=_row_reversal(th), th=th)

    fmat_p = dft_consts(tp)
    fmat_s = dft_consts(ts)
    wz = _fold_channel_dft(w_in[:, :, 2320:2576], _channel_dft())
    layers = [_layer_consts(l, w, wz) for l in range(DEPTH)]

    mods_p = [mod[l, bs:bs + 1].reshape(1, 1, 3 * D_MODEL) for l in range(DEPTH)]
    mods_s = [mod[l, 0:bs].reshape(bs, 1, 3 * D_MODEL) for l in range(DEPTH)]
    states_s = [state_delta[:, l] for l in range(DEPTH)]
    y_prompt, ctx_states = _trunk(x_prompt, None, mods_p, None, layers, shared, fmat_p, final_norm_g)
    y_sample, _ = _trunk(x_sample, _grid_pos_embed(ts), mods_s, states_s, layers, shared, fmat_s, final_norm_g)
    return y_prompt, y_sample, jnp.stack(ctx_states, axis=1)
```

```python
import functools
import math

import numpy as np
import jax
import jax.numpy as jnp
from jax import lax
from jax.experimental import pallas as pl
from jax.experimental.pallas import tpu as pltpu

F32 = jnp.float32
BF16 = jnp.bfloat16

D_MODEL = 1024
DEPTH = 4
GRID_W = 64
POS_BASE = 10000.0
H_A = 4
DK = 64
W_A = 256
CONV_W = 4
DN_CHUNK = 64
W_B = 256
G_B = 4
SGU_CHUNK = 128
W_C = 256
POOL_WINDOWS = (2, 4, 8, 16)
W_D = 256
G_D = 4
C_D = 64
EPS = 1e-6

P_MAIN = 3072
P_PAD = P_MAIN + 128
HALO = 16
DREC = 5 * W_A
SCAN_SEQS = 8
DELTA_TILE = 256

VMEM_LIMIT = 56 * 1024 * 1024


def _cparams(sem):
    return pltpu.CompilerParams(dimension_semantics=sem, vmem_limit_bytes=VMEM_LIMIT)


def _silu(x):
    return x * jax.nn.sigmoid(x)


def _split2(x):
    hi = x.astype(BF16)
    lo = (x - hi.astype(F32)).astype(BF16)
    return hi, lo


def _split3(x):
    hi = x.astype(BF16)
    r = x - hi.astype(F32)
    mid = r.astype(BF16)
    lo = (r - mid.astype(F32)).astype(BF16)
    return hi, mid, lo


def _dot(a, b):
    return jnp.dot(a, b, preferred_element_type=F32)


def _dot_nt(a, b):
    return lax.dot_general(a, b, (((1,), (1,)), ((), ())), preferred_element_type=F32)


def _dot_tn(a, b):
    return lax.dot_general(a, b, (((0,), (0,)), ((), ())), preferred_element_type=F32)


def _dot_exact_rhs(x, sel):
    hi, mid, lo = _split3(x)
    return _dot(hi, sel) + _dot(mid, sel) + _dot(lo, sel)


def _dot_exact_lhs(sel, x):
    hi, mid, lo = _split3(x)
    return _dot(sel, hi) + _dot(sel, mid) + _dot(sel, lo)


def _mod_kernel(c_ref, w_ref, b_ref, o_ref):
    a = _silu(c_ref[...]).astype(BF16)
    o_ref[0] = _dot(a, w_ref[0].astype(BF16)) + b_ref[0]


def _modulation(conds, ada_w, ada_b):
    nc = conds.shape[0]
    tn = 1024
    return pl.pallas_call(
        _mod_kernel,
        grid=(DEPTH, 3 * D_MODEL // tn),
        in_specs=[pl.BlockSpec((nc, D_MODEL), lambda l, n: (0, 0)),
                  pl.BlockSpec((1, D_MODEL, tn), lambda l, n: (l, 0, n)),
                  pl.BlockSpec((1, 1, tn), lambda l, n: (l, 0, n))],
        out_specs=pl.BlockSpec((1, nc, tn), lambda l, n: (l, 0, n)),
        out_shape=jax.ShapeDtypeStruct((DEPTH, nc, 3 * D_MODEL), F32),
        compiler_params=_cparams(("parallel", "parallel")),
        name="adaln_mod",
    )(conds, ada_w, ada_b.reshape(DEPTH, 1, 3 * D_MODEL))


def _inproj_body(x, mod_ref, ng_ref, w_ref, pa_ref, pb_ref, pc_ref, gd_ref, z_ref, ba_ref):
    ms = jnp.mean(x * x, axis=-1, keepdims=True)
    y = x * lax.rsqrt(ms + EPS) * ng_ref[...]
    shift = mod_ref[0, :, 0:D_MODEL]
    scale = mod_ref[0, :, D_MODEL:2 * D_MODEL]
    h = (y * (1.0 + scale) + shift).astype(BF16)
    pa_ref[0] = _dot(h, w_ref[:, 0:1024]).astype(BF16)
    pb_ref[0] = _dot(h, w_ref[:, 1024:1792]).astype(BF16)
    pc_ref[0] = _dot(h, w_ref[:, 1792:2304]).astype(BF16)
    gd_ref[0] = _dot(h, w_ref[:, 2304:2560]).astype(BF16)
    z = _dot(h, w_ref[:, 2560:3072])
    z_ref[0, 0] = z[:, 0:256].astype(BF16)
    z_ref[0, 1] = z[:, 256:512].astype(BF16)
    ba_ref[0] = _dot(h, w_ref[:, 3072:3200])


def _inproj_kernel(x_ref, mod_ref, ng_ref, w_ref, pa_ref, pb_ref, pc_ref, gd_ref, z_ref, ba_ref):
    _inproj_body(x_ref[0], mod_ref, ng_ref, w_ref, pa_ref, pb_ref, pc_ref, gd_ref, z_ref, ba_ref)


def _inproj_pos_kernel(x_ref, pe_ref, mod_ref, ng_ref, w_ref, x0_ref, pa_ref, pb_ref, pc_ref, gd_ref, z_ref, ba_ref):
    x = x_ref[0] + pe_ref[...]
    x0_ref[0] = x
    _inproj_body(x, mod_ref, ng_ref, w_ref, pa_ref, pb_ref, pc_ref, gd_ref, z_ref, ba_ref)


def _inproj(x, pe, mod, norm_g, w_in_p, tm):
    B, T, D = x.shape
    nmod = mod.shape[0]
    mod_map = (lambda b, i: (b, 0, 0)) if nmod == B else (lambda b, i: (0, 0, 0))
    tok = lambda n: pl.BlockSpec((1, tm, n), lambda b, i: (b, i, 0))
    in_specs = [pl.BlockSpec((1, 1, 3 * D), mod_map),
                pl.BlockSpec((1, D), lambda b, i: (0, 0)),
                pl.BlockSpec((D, P_PAD), lambda b, i: (0, 0))]
    out_specs = [tok(1024), tok(768), tok(512), tok(256),
                 pl.BlockSpec((1, 2, tm, 256), lambda b, i: (b, 0, i, 0)), tok(128)]
    out_shape = [jax.ShapeDtypeStruct((B, T, 1024), BF16),
                 jax.ShapeDtypeStruct((B, T, 768), BF16),
                 jax.ShapeDtypeStruct((B, T, 512), BF16),
                 jax.ShapeDtypeStruct((B, T, 256), BF16),
                 jax.ShapeDtypeStruct((B, 2, T, 256), BF16),
                 jax.ShapeDtypeStruct((B, T, 128), F32)]
    args = [mod, norm_g.reshape(1, D), w_in_p]
    if pe is None:
        kern, in_specs, args = _inproj_kernel, [tok(D)] + in_specs, [x] + args
    else:
        kern = _inproj_pos_kernel
        in_specs = [tok(D), pl.BlockSpec((tm, D), lambda b, i: (i, 0))] + in_specs
        args = [x, pe] + args
        out_specs = [tok(D)] + out_specs
        out_shape = [jax.ShapeDtypeStruct((B, T, D), F32)] + out_shape
    return pl.pallas_call(
        kern,
        grid=(B, T // tm),
        in_specs=in_specs,
        out_specs=out_specs,
        out_shape=out_shape,
        compiler_params=_cparams(("parallel", "parallel")),
        name="in_proj",
    )(*args)


def _fold_kernel(w_ref, cs_ref, o_ref):
    o_ref[0] = _dot_exact_rhs(w_ref[0], cs_ref[...])


def _fold_channel_dft(w_xd, cs):
    return pl.pallas_call(
        _fold_kernel,
        grid=(DEPTH,),
        in_specs=[pl.BlockSpec((1, D_MODEL, 256), lambda l: (l, 0, 0)),
                  pl.BlockSpec((256, 512), lambda l: (0, 0))],
        out_specs=pl.BlockSpec((1, D_MODEL, 512), lambda l: (l, 0, 0)),
        out_shape=jax.ShapeDtypeStruct((DEPTH, D_MODEL, 512), F32),
        compiler_params=_cparams(("parallel",)),
        name="fold_channel_dft",
    )(w_xd, cs)


def _outproj_body(of_ref, ob_ref, ga_ref, grp_ref, ng_ref, yb_ref, yc_ref, yd_ref, x_ref, mod_ref, w_ref):
    o = of_ref[0] + ob_ref[0]
    ms = _dot((o * o).astype(BF16), grp_ref[...]) * (1.0 / DK)
    ya = (o * lax.rsqrt(ms + EPS) * ng_ref[...] * _silu(ga_ref[0].astype(F32))).astype(BF16)
    ycat = jnp.concatenate([ya, yb_ref[0], yc_ref[0], yd_ref[0]], axis=-1)
    out = _dot(ycat, w_ref[...])
    gate = mod_ref[0, :, 2 * D_MODEL:3 * D_MODEL]
    return x_ref[0] + gate * out


def _outproj_final_kernel(of_ref, ob_ref, ga_ref, grp_ref, ng_ref, yb_ref, yc_ref, yd_ref, x_ref, mod_ref, w_ref,
                          fg_ref, o_ref):
    xn = _outproj_body(of_ref, ob_ref, ga_ref, grp_ref, ng_ref, yb_ref, yc_ref, yd_ref, x_ref, mod_ref, w_ref)
    ms = jnp.mean(xn * xn, axis=-1, keepdims=True)
    o_ref[0] = xn * lax.rsqrt(ms + EPS) * fg_ref[...]


def _outproj_inproj_kernel(of_ref, ob_ref, ga_ref, grp_ref, ng_ref, yb_ref, yc_ref, yd_ref, x_ref, mod_ref, w_ref,
                           modn_ref, ngn_ref, win_ref, xn_ref, pa_ref, pb_ref, pc_ref, gd_ref, z_ref, ba_ref):
    xn = _outproj_body(of_ref, ob_ref, ga_ref, grp_ref, ng_ref, yb_ref, yc_ref, yd_ref, x_ref, mod_ref, w_ref)
    xn_ref[0] = xn
    _inproj_body(xn, modn_ref, ngn_ref, win_ref, pa_ref, pb_ref, pc_ref, gd_ref, z_ref, ba_ref)


def _outproj_specs(x, mod, tm):
    B, T, D = x.shape
    mod_map = (lambda b, i: (b, 0, 0)) if mod.shape[0] == B else (lambda b, i: (0, 0, 0))
    tok = lambda n: pl.BlockSpec((1, tm, n), lambda b, i: (b, i, 0))
    in_specs = [tok(256), tok(256),
                pl.BlockSpec((1, tm, 256), lambda b, i: (b, i, 3)),
                pl.BlockSpec((256, 256), lambda b, i: (0, 0)),
                pl.BlockSpec((1, 256), lambda b, i: (0, 0)),
                tok(256), tok(256), tok(256), tok(D),
                pl.BlockSpec((1, 1, 3 * D), mod_map),
                pl.BlockSpec((D, D), lambda b, i: (0, 0))]
    return in_specs, tok, mod_map


def _outproj_final(o_f, o_b, pa, grp, ng4, yb, yc, yd, x, mod, w_out, final_g, tm):
    B, T, D = x.shape
    in_specs, tok, _ = _outproj_specs(x, mod, tm)
    return pl.pallas_call(
        _outproj_final_kernel,
        grid=(B, T // tm),
        in_specs=in_specs + [pl.BlockSpec((1, D), lambda b, i: (0, 0))],
        out_specs=tok(D),
        out_shape=jax.ShapeDtypeStruct((B, T, D), F32),
        compiler_params=_cparams(("parallel", "parallel")),
        name="out_proj",
    )(o_f, o_b, pa, grp, ng4, yb, yc, yd, x, mod, w_out, final_g.reshape(1, D))


def _outproj_inproj(o_f, o_b, pa, grp, ng4, yb, yc, yd, x, mod, w_out, norm_g_next, w_in_next, tm):
    B, T, D = x.shape
    mod_cur, mod_next = mod
    in_specs, tok, mod_map = _outproj_specs(x, mod_cur, tm)
    in_specs += [pl.BlockSpec((1, 1, 3 * D), mod_map),
                 pl.BlockSpec((1, D), lambda b, i: (0, 0)),
                 pl.BlockSpec((D, P_PAD), lambda b, i: (0, 0))]
    out_specs = [tok(D), tok(1024), tok(768), tok(512), tok(256),
                 pl.BlockSpec((1, 2, tm, 256), lambda b, i: (b, 0, i, 0)), tok(128)]
    out_shape = [jax.ShapeDtypeStruct((B, T, D), F32),
                 jax.ShapeDtypeStruct((B, T, 1024), BF16),
                 jax.ShapeDtypeStruct((B, T, 768), BF16),
                 jax.ShapeDtypeStruct((B, T, 512), BF16),
                 jax.ShapeDtypeStruct((B, T, 256), BF16),
                 jax.ShapeDtypeStruct((B, 2, T, 256), BF16),
                 jax.ShapeDtypeStruct((B, T, 128), F32)]
    return pl.pallas_call(
        _outproj_inproj_kernel,
        grid=(B, T // tm),
        in_specs=in_specs,
        out_specs=out_specs,
        out_shape=out_shape,
        compiler_params=_cparams(("parallel", "parallel")),
        name="out_in_proj",
    )(o_f, o_b, pa, grp, ng4, yb, yc, yd, x, mod_cur, w_out, mod_next, norm_g_next.reshape(1, D), w_in_next)


def _gelu_tanh(x):
    return 0.5 * x * (1.0 + jnp.tanh(math.sqrt(2.0 / math.pi) * (x + 0.044715 * (x * x * x))))


def _diag_blocks(r, lane_grp):
    n = r.shape[0] // 4
    out = r[0:n]
    for g in range(1, 4):
        out = jnp.where(lane_grp == g, r[g * n:(g + 1) * n], out)
    return out


def _local_kernel(pb_ref, pc_ref, prev_ref, next_ref, sgw_ref, sgb_ref, sgn_ref, band_ref, pw_ref, ps_ref,
                  yb_ref, yc_ref, *, T, tl):
    i = pl.program_id(1)
    nblk = T // tl
    n = SGU_CHUNK
    lane_grp = lax.broadcasted_iota(jnp.int32, (n, 256), 1) // 64
    row = lax.broadcasted_iota(jnp.int32, (n, 256), 0)
    lo_w = jnp.zeros((n, 256), jnp.int32)
    hi_w = jnp.zeros((n, 256), jnp.int32)
    for g, w in enumerate(POOL_WINDOWS):
        lo_w = jnp.where(lane_grp == g, w // 2, lo_w)
        hi_w = jnp.where(lane_grp == g, w - w // 2, hi_w)
    prev_blk = jnp.where(i > 0, prev_ref[0, :, 0:256], jnp.zeros((HALO, 256), BF16))
    next_blk = jnp.where(i < nblk - 1, next_ref[0, :, 0:256], jnp.zeros((HALO, 256), BF16))
    nch = tl // n
    for j in range(nch):
        r0 = j * n
        uvg = pb_ref[0, r0:r0 + n, :]
        gel = _gelu_tanh(uvg[:, 0:512].astype(F32))
        u = gel[:, 0:256]
        v = gel[:, 256:512]
        vn = v * lax.rsqrt(jnp.mean(v * v, axis=-1, keepdims=True) + EPS) * sgn_ref[...]
        sv = _diag_blocks(_dot(sgw_ref[...], vn.astype(BF16)), lane_grp) + sgb_ref[...]
        yb_ref[0, r0:r0 + n, :] = (u * sv * _silu(uvg[:, 512:768].astype(F32))).astype(BF16)
        xt = pc_ref[0, r0:r0 + n, 0:256]
        pv = prev_blk if j == 0 else pc_ref[0, r0 - HALO:r0, 0:256]
        nx = next_blk if j == nch - 1 else pc_ref[0, r0 + n:r0 + n + HALO, 0:256]
        xcat = jnp.concatenate([pv, xt, nx], axis=0)
        wsum = _diag_blocks(_dot(band_ref[...], xcat), lane_grp)
        t = i * tl + r0 + row
        cnt = (jnp.minimum(t + hi_w, T) - jnp.maximum(t - lo_w, 0)).astype(F32)
        pooled = wsum / cnt - xt.astype(F32)
        yc = _dot(pooled.astype(BF16), pw_ref[...]) * ps_ref[...]
        yc_ref[0, r0:r0 + n, :] = (yc * _silu(pc_ref[0, r0:r0 + n, 256:512].astype(F32))).astype(BF16)


def _local(pb, pc, sgw, sgb, sgn, band, pw_bd, pscale, tl):
    B, T, _ = pb.shape
    hb = tl // HALO
    nh = T // HALO
    const = lambda shape: pl.BlockSpec(shape, lambda b, i: (0,) * len(shape))
    return pl.pallas_call(
        functools.partial(_local_kernel, T=T, tl=tl),
        grid=(B, T // tl),
        in_specs=[pl.BlockSpec((1, tl, 768), lambda b, i: (b, i, 0)),
                  pl.BlockSpec((1, tl, 512), lambda b, i: (b, i, 0)),
                  pl.BlockSpec((1, HALO, 512), lambda b, i: (b, jnp.maximum(i * hb - 1, 0), 0)),
                  pl.BlockSpec((1, HALO, 512), lambda b, i: (b, jnp.minimum((i + 1) * hb, nh - 1), 0)),
                  const((4 * SGU_CHUNK, SGU_CHUNK)), const((SGU_CHUNK, 256)), const((1, 256)),
                  const((4 * SGU_CHUNK, SGU_CHUNK + 2 * HALO)), const((256, 256)), const((1, 256))],
        out_specs=[pl.BlockSpec((1, tl, 256), lambda b, i: (b, i, 0)),
                   pl.BlockSpec((1, tl, 256), lambda b, i: (b, i, 0))],
        out_shape=[jax.ShapeDtypeStruct((B, T, 256), BF16), jax.ShapeDtypeStruct((B, T, 256), BF16)],
        compiler_params=_cparams(("parallel", "parallel")),
        name="sgu_pool",
    )(pb, pc, pc, pc, sgw, sgb, sgn, band, pw_bd, pscale)


DFT_EXTRA = 8


def _dft_kernel(f_ref, z_ref, glo_ref, ghi_ref, rev_ref, fw_ref, ylo_ref, yhi_ref, *, T, th, inv_norm):
    p = _dot(f_ref[:, 0:T], z_ref[0, 0:T, :])
    q = _dot(f_ref[:, T:2 * T], z_ref[0, T:2 * T, :])
    f_lo = ((p + q) * inv_norm)[0:th].astype(BF16)
    f_hi = ((p - q) * inv_norm)[1:th + 1].astype(BF16)
    f_hi = _dot(rev_ref[...], f_hi).astype(BF16)
    ylo_ref[0] = (_dot(f_lo, fw_ref[...]) * _silu(glo_ref[0].astype(F32))).astype(BF16)
    yhi_ref[0] = (_dot(f_hi, fw_ref[...]) * _silu(ghi_ref[0].astype(F32))).astype(BF16)


def _fourier(z, gd, fmat, rev, fw_bd, th):
    B, T, _ = gd.shape
    nh = T // 2 // th
    z = z.reshape(B, 2 * T, 256)
    y_lo, y_hi = pl.pallas_call(
        functools.partial(_dft_kernel, T=T, th=th, inv_norm=1.0 / math.sqrt(T * C_D)),
        grid=(nh, B),
        in_specs=[pl.BlockSpec((th + DFT_EXTRA, 2 * T), lambda i, b: (i, 0)),
                  pl.BlockSpec((1, 2 * T, 256), lambda i, b: (b, 0, 0)),
                  pl.BlockSpec((1, th, 256), lambda i, b: (b, i, 0)),
                  pl.BlockSpec((1, th, 256), lambda i, b: (b, 2 * nh - 1 - i, 0)),
                  pl.BlockSpec((th, th), lambda i, b: (0, 0)),
                  pl.BlockSpec((256, 256), lambda i, b: (0, 0))],
        out_specs=[pl.BlockSpec((1, th, 256), lambda i, b: (b, i, 0)),
                   pl.BlockSpec((1, th, 256), lambda i, b: (b, nh - 1 - i, 0))],
        out_shape=[jax.ShapeDtypeStruct((B, T // 2, 256), BF16), jax.ShapeDtypeStruct((B, T // 2, 256), BF16)],
        compiler_params=_cparams(("parallel", "parallel")),
        name="dft_positions",
    )(fmat, z, gd, gd, rev, fw_bd)
    return jnp.concatenate([y_lo, y_hi], axis=1)


TRI_UNITS = 128
TRI_STRIDE = DN_CHUNK + 8


def _expand(x, bd):
    return jnp.concatenate([x, x, x, x], axis=0) * bd


def _delta_prep_kernel(qkv_ref, prev_ref, next_ref, ba_ref, cw_ref, par_ref, sel_ref, grp_ref, tri_ref,
                       d0_ref, d1_ref, a_ref, gl_ref, *, T, tc):
    j = pl.program_id(1)
    nblk = T // tc
    x = qkv_ref[0].astype(F32)
    prev = jnp.where(j > 0, prev_ref[0].astype(F32), 0.0)
    nxt = jnp.where(j < nblk - 1, next_ref[0].astype(F32), 0.0)
    xcat = jnp.concatenate([prev, x, nxt], axis=0)
    n = tc + 2 * HALO
    acc = None
    for tap in range(CONV_W):
        s = (CONV_W // 2 - tap) % n
        xs = xcat if s == 0 else pltpu.roll(xcat, s, axis=0)
        term = xs[HALO:HALO + tc] * cw_ref[tap:tap + 1, :]
        acc = term if acc is None else acc + term
    c = _silu(acc)
    cq, ck, v = c[:, 0:256], c[:, 256:512], c[:, 512:768]
    grp = grp_ref[...]
    q = cq * lax.rsqrt(_dot((cq * cq).astype(BF16), grp) + EPS) * (DK ** -0.5)
    k = ck * lax.rsqrt(_dot((ck * ck).astype(BF16), grp) + EPS)
    ba = ba_ref[0].T[0:4 * H_A]
    z = ba + par_ref[1]
    softplus = jnp.maximum(z, 0.0) + jnp.log1p(jnp.exp(-jnp.abs(z)))
    row = lax.broadcasted_iota(jnp.int32, ba.shape, 0)
    bg = jnp.where(row < 2 * H_A, jax.nn.sigmoid(ba), -jnp.exp(par_ref[0]) * softplus)
    gsum_f = _dot_exact_rhs(bg, tri_ref[1])
    gsum_b = _dot_exact_rhs(bg, tri_ref[0])
    narrow = jnp.where(row < 2 * H_A, bg, jnp.where(row < 3 * H_A, gsum_f, gsum_b))
    hi, mid, lo = _split3(narrow)
    sel = sel_ref[...]
    bgx = _dot_tn(hi, sel) + _dot_tn(mid, sel) + _dot_tn(lo, sel)
    gcum = (bgx[:, 512:768], bgx[:, 768:1024])

    bd = grp
    r64 = lax.broadcasted_iota(jnp.int32, (DN_CHUNK, 256), 0)
    c64 = lax.broadcasted_iota(jnp.int32, (DN_CHUNK, 256), 1) % 64
    diag64 = r64 == c64
    pad_rows = jnp.zeros((TRI_STRIDE - DN_CHUNK, 128), F32)
    out_refs = (d0_ref, d1_ref)
    for ch in range(tc // DN_CHUNK):
        r0 = ch * DN_CHUNK
        a0 = ch * TRI_STRIDE
        kx = k[r0:r0 + DN_CHUNK]
        qx = q[r0:r0 + DN_CHUNK]
        vx = v[r0:r0 + DN_CHUNK]
        kb = kx.astype(BF16)
        kq = _dot_nt(jnp.concatenate([kb, qx.astype(BF16)], axis=0), _expand(kb, bd))
        kk, qk = kq[0:DN_CHUNK], kq[DN_CHUNK:2 * DN_CHUNK]
        for d in range(2):
            gc = gcum[d][r0:r0 + DN_CHUNK]
            beta = bgx[r0:r0 + DN_CHUNK, d * 256:(d + 1) * 256]
            rowv = jnp.sum(jnp.where(diag64, gc, 0.0), axis=0, keepdims=True)
            incl = (r64 >= c64) if d == 0 else (r64 <= c64)
            strict = (r64 > c64) if d == 0 else (r64 < c64)
            decay = jnp.where(incl, jnp.exp(jnp.where(incl, gc - rowv, 0.0)), 0.0)
            a_c = jnp.where(strict, beta * kk * decay, 0.0)
            for half in range(2):
                a_ref[d, half, a0:a0 + DN_CHUNK, :] = a_c[:, 128 * half:128 * half + 128]
                a_ref[d, half, a0 + DN_CHUNK:a0 + TRI_STRIDE, :] = pad_rows
            egc = jnp.exp(gc)
            g_last = gc[DN_CHUNK - 1:DN_CHUNK] if d == 0 else gc[0:1]
            o = out_refs[d]
            o[0, r0:r0 + DN_CHUNK, 0:256] = (vx * beta).astype(BF16)
            o[0, r0:r0 + DN_CHUNK, 256:512] = (kx * beta * egc).astype(BF16)
            o[0, r0:r0 + DN_CHUNK, 512:768] = (qx * egc).astype(BF16)
            o[0, r0:r0 + DN_CHUNK, 768:1024] = (kx * jnp.exp(g_last - gc)).astype(BF16)
            o[0, r0:r0 + DN_CHUNK, 1024:1280] = (qk * decay).astype(BF16)
            gl_ref[0, d, ch] = jnp.exp(g_last)


def _delta_prep(pa, ba, conv_w, par, sel, grp, tri, tc):
    B, T, _ = pa.shape
    hb = tc // HALO
    nh = T // HALO
    nt = T // tc
    nch = tc // DN_CHUNK
    const = lambda shape: pl.BlockSpec(shape, lambda b, i: (0,) * len(shape))
    return pl.pallas_call(
        functools.partial(_delta_prep_kernel, T=T, tc=tc),
        grid=(B, nt),
        in_specs=[pl.BlockSpec((1, tc, 768), lambda b, i: (b, i, 0)),
                  pl.BlockSpec((1, HALO, 768), lambda b, i: (b, jnp.maximum(i * hb - 1, 0), 0)),
                  pl.BlockSpec((1, HALO, 768), lambda b, i: (b, jnp.minimum((i + 1) * hb, nh - 1), 0)),
                  pl.BlockSpec((1, tc, 128), lambda b, i: (b, i, 0)),
                  const((CONV_W, 768)), const((2, 4 * H_A, tc)), const((4 * H_A, 1024)), const((256, 256)),
                  const((2, tc, tc))],
        out_specs=[pl.BlockSpec((1, tc, DREC), lambda b, i: (b, i, 0)),
                   pl.BlockSpec((1, tc, DREC), lambda b, i: (b, i, 0)),
                   pl.BlockSpec((2, 2, nch * TRI_STRIDE, 128), lambda b, i: (0, 0, b * nt + i, 0)),
                   pl.BlockSpec((1, 2, nch, 1, 256), lambda b, i: (b, 0, i, 0, 0))],
        out_shape=[jax.ShapeDtypeStruct((B, T, DREC), BF16),
                   jax.ShapeDtypeStruct((B, T, DREC), BF16),
                   jax.ShapeDtypeStruct((2, 2, B * nt * nch * TRI_STRIDE, 128), F32),
                   jax.ShapeDtypeStruct((B, 2, T // DN_CHUNK, 1, 256), F32)],
        compiler_params=_cparams(("parallel", "parallel")),
        name="delta_prep",
    )(pa, pa, pa, ba, conv_w, par, sel, grp, tri)


TRI_ROWS = TRI_UNITS * TRI_STRIDE


def _tri_rows(x_ref, lower):
    sub = lax.broadcasted_iota(jnp.int32, (8, TRI_UNITS), 0)
    nb = DN_CHUNK // 8
    for ib in (range(nb) if lower else reversed(range(nb))):
        segs = range(ib + 1) if lower else range(ib, nb)
        live = range(ib + 1) if lower else range(ib, nb)

        def row(step, carry, ib=ib, segs=segs, live=live):
            i = 8 * ib + (step if lower else 7 - step)
            for heads in ((0, 1), (2, 3)):
                acc = {(h, cv): jnp.zeros((8, TRI_UNITS), F32) for h in heads for cv in live}
                for jb in segs:
                    cvs = range(jb + 1) if lower else range(jb, nb)
                    for j in range(8 * jb, 8 * jb + 8):
                        for h in heads:
                            coef = jnp.broadcast_to(x_ref[i, pl.ds(h * DK + j, 1), :], (8, TRI_UNITS))
                            for cv in cvs:
                                acc[h, cv] = acc[h, cv] + coef * x_ref[j, h * DK + 8 * cv:h * DK + 8 * cv + 8, :]
                for h in heads:
                    for cv in range(nb):
                        val = (jnp.where(sub + 8 * cv == i, 1.0, -acc[h, cv]) if cv in live
                               else jnp.zeros((8, TRI_UNITS), F32))
                        x_ref[i, h * DK + 8 * cv:h * DK + 8 * cv + 8, :] = val
            return carry

        lax.fori_loop(0, 8, row, 0)


def _tri_inverse_kernel(a_ref, t_ref, x_ref):
    d = pl.program_id(0)

    def load(i, carry):
        for half in range(2):
            x_ref[i, 128 * half:128 * half + 128, :] = a_ref[0, half, pl.ds(i, TRI_UNITS, stride=TRI_STRIDE), :].T
        return carry

    lax.fori_loop(0, DN_CHUNK, load, 0, unroll=4)

    @pl.when(d == 0)
    def _():
        _tri_rows(x_ref, True)

    @pl.when(d == 1)
    def _():
        _tri_rows(x_ref, False)

    def store(i, carry):
        for half in range(2):
            t_ref[0, half, pl.ds(i, TRI_UNITS, stride=TRI_STRIDE), :] = x_ref[i, 128 * half:128 * half + 128, :].T
        return carry

    lax.fori_loop(0, DN_CHUNK, store, 0, unroll=4)
    for half in range(2):
        for p in range(DN_CHUNK, TRI_STRIDE):
            t_ref[0, half, pl.ds(p, TRI_UNITS, stride=TRI_STRIDE), :] = jnp.zeros((TRI_UNITS, 128), F32)


def _tri_inverse(a):
    rows = a.shape[2]
    spec = pl.BlockSpec((1, 2, TRI_ROWS, 128), lambda d, i: (d, 0, i, 0))
    return pl.pallas_call(
        _tri_inverse_kernel,
        grid=(2, rows // TRI_ROWS),
        in_specs=[spec],
        out_specs=spec,
        out_shape=jax.ShapeDtypeStruct(a.shape, F32),
        scratch_shapes=[pltpu.VMEM((DN_CHUNK, 256, TRI_UNITS), F32)],
        compiler_params=_cparams(("parallel", "parallel")),
        name="tri_inverse",
    )(a)


def _delta_scan_kernel_with_state(d0_ref, d1_ref, tf_ref, tb_ref, glf_ref, glb_ref, grp_ref, s0_ref, of_ref, ob_ref,
                                  sfin_ref, s_ref, *, tc, nb):
    _delta_scan_kernel(s0_ref, d0_ref, d1_ref, tf_ref, tb_ref, glf_ref, glb_ref, grp_ref, of_ref, ob_ref, sfin_ref,
                       s_ref, tc=tc, nb=nb)


def _delta_scan_kernel(s0_ref, d0_ref, d1_ref, tf_ref, tb_ref, glf_ref, glb_ref, grp_ref, of_ref, ob_ref, sfin_ref,
                       s_ref, *, tc, nb):
    j = pl.program_id(1)
    nblk = pl.num_programs(1)
    nch = tc // DN_CHUNK

    @pl.when(j == 0)
    def _():
        if s0_ref is None:
            s_ref[...] = jnp.zeros(s_ref.shape, F32)
        else:
            zero = jnp.zeros((DK, DK), F32)
            for q in range(nb):
                for d in range(2):
                    s_ref[q, d] = jnp.concatenate(
                        [jnp.concatenate([s0_ref[q, d, h] if g == h else zero for g in range(H_A)], axis=1)
                         for h in range(H_A)], axis=0)

    bd = grp_ref[...]
    bd_f32 = bd.astype(F32)
    rec = (d0_ref, d1_ref)
    tri = (tf_ref, tb_ref)
    glr = (glf_ref, glb_ref)
    out = (of_ref, ob_ref)
    chains = [(q, d) for q in range(nb) for d in range(2)]

    def chunk_of(d, step):
        return step if d == 0 else nch - 1 - step

    def rows_of(c):
        return slice(c * DN_CHUNK, (c + 1) * DN_CHUNK)

    uw = {}
    for step in range(nch):
        for q, d in chains:
            c = chunk_of(d, step)
            trows = slice(c * TRI_STRIDE, c * TRI_STRIDE + DN_CHUNK)
            tinv = jnp.concatenate([tri[d][0, 0, q, trows, :], tri[d][0, 1, q, trows, :]], axis=1).astype(BF16)
            rhs = jnp.concatenate([_expand(rec[d][q, rows_of(c), 0:256], bd),
                                   _expand(rec[d][q, rows_of(c), 256:512], bd)], axis=1)
            uw[q, d, step] = _dot(tinv, rhs)

    for step in range(nch):
        s = {k: s_ref[k[0], k[1]] for k in chains}
        ws_qs = {}
        for q, d in chains:
            r = rows_of(chunk_of(d, step))
            wq = jnp.concatenate([uw[q, d, step][:, 256:512].astype(BF16), rec[d][q, r, 512:768]], axis=0)
            ws_qs[q, d] = _dot(wq, s[q, d].astype(BF16))
        v_new = {k: (uw[k[0], k[1], step][:, 0:256] - ws_qs[k][0:DN_CHUNK]).astype(BF16) for k in chains}
        for q, d in chains:
            c = chunk_of(d, step)
            r = rows_of(c)
            out[d][q, r, :] = ws_qs[q, d][DN_CHUNK:2 * DN_CHUNK] + _dot(rec[d][q, r, 1024:1280], _expand(v_new[q, d], bd))
            upd = _dot_tn(rec[d][q, r, 768:1024], v_new[q, d])
            s_ref[q, d] = s[q, d] * glr[d][q, 0, c] + upd * bd_f32

    @pl.when(j == nblk - 1)
    def _():
        for q in range(nb):
            for d in range(2):
                s = s_ref[q, d]
                for h in range(H_A):
                    sfin_ref[q, d, h] = s[h * DK:(h + 1) * DK, h * DK:(h + 1) * DK]


def _delta_scan(d0, d1, tinv, gl, grp, s0, tc, nb):
    B, T, _ = d0.shape
    nblk = T // tc
    nch = tc // DN_CHUNK
    tinv = tinv.reshape(2, 2, B, nblk * nch * TRI_STRIDE, 128)
    state_spec = pl.BlockSpec((nb, 2, H_A, DK, DK), lambda b, j: (b, 0, 0, 0, 0))
    in_specs = [pl.BlockSpec((nb, tc, DREC), lambda b, j: (b, j, 0)),
                pl.BlockSpec((nb, tc, DREC), lambda b, j: (b, nblk - 1 - j, 0)),
                pl.BlockSpec((1, 2, nb, nch * TRI_STRIDE, 128), lambda b, j: (0, 0, b, j, 0)),
                pl.BlockSpec((1, 2, nb, nch * TRI_STRIDE, 128), lambda b, j: (1, 0, b, nblk - 1 - j, 0)),
                pl.BlockSpec((nb, 1, nch, 1, 256), lambda b, j: (b, 0, j, 0, 0)),
                pl.BlockSpec((nb, 1, nch, 1, 256), lambda b, j: (b, 1, nblk - 1 - j, 0, 0)),
                pl.BlockSpec((256, 256), lambda b, j: (0, 0))]
    args = [d0, d1, tinv, tinv, gl, gl, grp]
    if s0 is None:
        kern = functools.partial(_delta_scan_kernel, None, tc=tc, nb=nb)
    else:
        kern = functools.partial(_delta_scan_kernel_with_state, tc=tc, nb=nb)
        in_specs.append(state_spec)
        args.append(s0)
    return pl.pallas_call(
        kern,
        grid=(B // nb, nblk),
        in_specs=in_specs,
        out_specs=[pl.BlockSpec((nb, tc, 256), lambda b, j: (b, j, 0)),
                   pl.BlockSpec((nb, tc, 256), lambda b, j: (b, nblk - 1 - j, 0)),
                   state_spec],
        out_shape=[jax.ShapeDtypeStruct((B, T, 256), F32),
                   jax.ShapeDtypeStruct((B, T, 256), F32),
                   jax.ShapeDtypeStruct((B, 2, H_A, DK, DK), F32)],
        scratch_shapes=[pltpu.VMEM((nb, 2, 256, 256), F32)],
        compiler_params=_cparams(("parallel", "arbitrary")),
        name="delta_scan",
    )(*args)


def _block_diag4(w):
    n = w.shape[-1]
    eye = jnp.eye(4, dtype=w.dtype)
    return jnp.einsum("gij,gh->gihj", w, eye).reshape(4 * n, 4 * n)


def _dft_matrix(T, th):
    w = 2.0 * math.pi / T
    nh = T // 2 // th
    groups = (th + DFT_EXTRA) // 8
    t = jnp.arange(T, dtype=jnp.int32)[None, None, :]
    coarse = (th * jnp.arange(nh, dtype=jnp.int32)[:, None] + 8 * jnp.arange(groups, dtype=jnp.int32)[None, :])
    coarse = coarse.reshape(nh * groups, 1, 1)
    fine = jnp.arange(8, dtype=jnp.int32)[None, :, None]
    ang_a = ((coarse * t) % T).astype(F32) * w
    ang_b = ((fine * t) % T).astype(F32) * w
    ca, sa, cb, sb = jnp.cos(ang_a), jnp.sin(ang_a), jnp.cos(ang_b), jnp.sin(ang_b)
    rows = nh * (th + DFT_EXTRA)
    cos_m = (ca * cb - sa * sb).reshape(rows, T)
    sin_m = (sa * cb + ca * sb).reshape(rows, T)
    return jnp.concatenate([cos_m, sin_m], axis=1).astype(BF16)


def _row_reversal(n):
    r = jnp.arange(n)
    return (r[:, None] + r[None, :] == n - 1).astype(BF16)


def _channel_dft():
    c = jnp.arange(C_D, dtype=jnp.int32)
    ang = ((c[:, None] * c[None, :]) % C_D).astype(F32) * (2.0 * math.pi / C_D)
    cos4 = _block_diag4(jnp.broadcast_to(jnp.cos(ang), (G_D, C_D, C_D)))
    sin4 = _block_diag4(jnp.broadcast_to(jnp.sin(ang), (G_D, C_D, C_D)))
    return jnp.concatenate([cos4, -sin4], axis=1).astype(BF16)


def _pool_band():
    n = SGU_CHUNK
    t = jnp.arange(n)[:, None]
    s = jnp.arange(n + 2 * HALO)[None, :] - HALO
    blocks = [((s >= t - w // 2) & (s < t + w - w // 2)) for w in POOL_WINDOWS]
    return jnp.concatenate(blocks, axis=0).astype(BF16)


def _chunk_triangles(tc):
    r = jnp.arange(tc)[:, None]
    c = jnp.arange(tc)[None, :]
    same = (r // DN_CHUNK) == (c // DN_CHUNK)
    return jnp.stack([same & (r >= c), same & (r <= c)]).astype(BF16)


def _gate_selector():
    src = jnp.arange(4 * H_A)[:, None]
    dst = jnp.arange(1024)[None, :]
    kind, d, h = dst // 512, (dst // 256) % 2, (dst // 64) % 4
    return (src == kind * 8 + d * 4 + h).astype(BF16)


def _grid_pos_embed(T):
    rows = T // GRID_W
    r = jnp.repeat(jnp.arange(rows, dtype=F32), GRID_W)
    col = jnp.tile(jnp.arange(GRID_W, dtype=F32), rows)
    n_freq = D_MODEL // 4
    freqs = jnp.power(POS_BASE, -jnp.arange(n_freq, dtype=F32) / n_freq)
    ar = r[:, None] * freqs[None]
    ac = col[:, None] * freqs[None]
    return jnp.concatenate([jnp.sin(ar), jnp.cos(ar), jnp.sin(ac), jnp.cos(ac)], axis=-1)


def _permute_w_in(win, wz):
    pad = jnp.zeros((DEPTH, D_MODEL, 128 - 4 * H_A), F32)
    return jnp.concatenate([win[:, :, 0:1024], win[:, :, 1040:2320], win[:, :, 2576:2832], wz, win[:, :, 1024:1040],
                            pad], axis=2).astype(BF16)


def _layer_consts(l, w, w_in_p_all):
    w_in_p = w_in_p_all[l]
    par = jnp.stack([jnp.concatenate([jnp.zeros((2 * H_A,), F32), w[name][l].reshape(2 * H_A)])
                     for name in ("a_log", "dt_bias")])
    par = jnp.broadcast_to(par[:, :, None], (2, 4 * H_A, DELTA_TILE))
    return dict(
        norm_g=w["norm_g"][l], w_in_p=w_in_p, conv_w=w["conv_qkv"][l], par=par,
        dn_g4=jnp.tile(w["dn_norm_g"][l], H_A).reshape(1, 256),
        sgw=w["sgu_w"][l].reshape(G_B * SGU_CHUNK, SGU_CHUNK).astype(BF16),
        sgb=jnp.repeat(w["sgu_b"][l].T, W_B // G_B, axis=1),
        sgn=w["sgu_norm_g"][l].reshape(1, W_B),
        pw_bd=_block_diag4(w["pool_w"][l]).astype(BF16),
        pscale=w["pool_scale"][l].reshape(1, W_C),
        fw_bd=_block_diag4(w["fourier_w"][l]).astype(BF16),
        w_out=w["w_out"][l].astype(BF16),
    )


def _trunk(x, pe, mods, states, layers, shared, fmat, final_g):
    B, T, _ = x.shape
    tm = min(T, 512)
    proj = _inproj(x, pe, mods[0], layers[0]["norm_g"], layers[0]["w_in_p"], tm)
    if pe is not None:
        x, proj = proj[0], proj[1:]
    finals = []
    for l, lc in enumerate(layers):
        pa, pb, pc, gd, z, ba = proj
        d0, d1, a_tri, gl = _delta_prep(pa, ba, lc["conv_w"], lc["par"], shared["sel"], shared["grp"], shared["tri"],
                                        DELTA_TILE)
        s0 = None if states is None else states[l]
        o_f, o_b, s_fin = _delta_scan(d0, d1, _tri_inverse(a_tri), gl, shared["grp"], s0, DELTA_TILE, SCAN_SEQS)
        finals.append(s_fin)
        yb, yc = _local(pb, pc, lc["sgw"], lc["sgb"], lc["sgn"], shared["band"], lc["pw_bd"], lc["pscale"], tm)
        yd = _fourier(z, gd, fmat["cs"], fmat["rev"], lc["fw_bd"], fmat["th"])
        mix = (o_f, o_b, pa, shared["grp"], lc["dn_g4"], yb, yc, yd, x)
        if l + 1 < len(layers):
            nxt = layers[l + 1]
            x, *proj = _outproj_inproj(*mix, (mods[l], mods[l + 1]), lc["w_out"], nxt["norm_g"], nxt["w_in_p"], tm)
        else:
            x = _outproj_final(*mix, mods[l], lc["w_out"], final_g, tm)
    return x, finals


def kernel(x_prompt, x_sample, state_delta, c, c_ctx, ada_w, ada_b, norm_g, w_in, conv_qkv, a_log, dt_bias,
           dn_norm_g, sgu_norm_g, sgu_w, sgu_b, pool_w, pool_scale, fourier_w, w_out, final_norm_g):
    w = dict(norm_g=norm_g, w_in=w_in, conv_qkv=conv_qkv, a_log=a_log, dt_bias=dt_bias, dn_norm_g=dn_norm_g,
             sgu_norm_g=sgu_norm_g, sgu_w=sgu_w, sgu_b=sgu_b, pool_w=pool_w, pool_scale=pool_scale,
             fourier_w=fourier_w, w_out=w_out)
    bp, tp, _ = x_prompt.shape
    bs, ts, _ = x_sample.shape
    conds = jnp.concatenate([c, c_ctx[None], jnp.zeros((16 - bs - 1, D_MODEL), F32)], axis=0)
    mod = _modulation(conds, ada_w, ada_b)
    ones64 = jnp.ones((H_A, DK, DK), F32)
    shared = dict(sel=_gate_selector(), grp=_block_diag4(ones64).astype(BF16), band=_pool_band(),
                  tri=_chunk_triangles(DELTA_TILE))
    def dft_consts(T):
        th = min(T // 2, 512)
        return dict(cs=_dft_matrix(T, th), rev=_row_reversal(th), th=th)

    fmat_p = dft_consts(tp)
    fmat_s = dft_consts(ts)
    wz = _fold_channel_dft(w_in[:, :, 2320:2576], _channel_dft())
    w_in_p_all = _permute_w_in(w_in, wz)
    layers = [_layer_consts(l, w, w_in_p_all) for l in range(DEPTH)]

    mods_p = [mod[l, bs:bs + 1].reshape(1, 1, 3 * D_MODEL) for l in range(DEPTH)]
    mods_s = [mod[l, 0:bs].reshape(bs, 1, 3 * D_MODEL) for l in range(DEPTH)]
    states_s = [state_delta[:, l] for l in range(DEPTH)]
    y_prompt, ctx_states = _trunk(x_prompt, None, mods_p, None, layers, shared, fmat_p, final_norm_g)
    y_sample, _ = _trunk(x_sample, _grid_pos_embed(ts), mods_s, states_s, layers, shared, fmat_s, final_norm_g)
    return y_prompt, y_sample, jnp.stack(ctx_states, axis=1)
```
